```python
import jax, jax.numpy as jnp
from jax import lax
import numpy as np

D_MODEL = 1024
BATCH = 8
SEQ = 8192
DEPTH = 2

N_META = 16
D_MIX = D_MODEL
ATTN_HEADS = 8
HEAD_DIM = 64
D_ATTN = ATTN_HEADS * HEAD_DIM
D_CONF = D_MIX // 4
D_SC = D_MIX - D_ATTN - D_CONF
CONF_KERNEL = 31
SC_KERNEL = 3
D_FF = 4 * D_MODEL
Q_BLOCK = 128
EPS = 1e-6
N_IN = 3 * D_ATTN + ATTN_HEADS + 2 * D_CONF + 3 * D_SC

kernel_name = 'hybrid_fox_conformer_shortconv_block'


def _rmsnorm(x, g):
    xf = x.astype(jnp.float32)
    y = xf * lax.rsqrt(jnp.mean(xf * xf, axis=-1, keepdims=True) + EPS)
    return (y * g.astype(jnp.float32)).astype(x.dtype)


def _layernorm(x, g, b):
    xf = x.astype(jnp.float32)
    mu = jnp.mean(xf, axis=-1, keepdims=True)
    xc = xf - mu
    y = xc * lax.rsqrt(jnp.mean(xc * xc, axis=-1, keepdims=True) + EPS)
    return (y * g.astype(jnp.float32) + b.astype(jnp.float32)).astype(x.dtype)


def _causal_dwconv(x, w):
    K, C = w.shape
    return lax.conv_general_dilated(
        x, w[:, None, :].astype(x.dtype), window_strides=(1,), padding=[(K - 1, 0)],
        dimension_numbers=('NWC', 'WIO', 'NWC'), feature_group_count=C)


def _fox_attend(q_blk, c_q, q_pos, k, v, c_k, k_pos):
    s = jnp.einsum('bhqd,bhkd->bhqk', q_blk, k, preferred_element_type=jnp.float32) * (HEAD_DIM ** -0.5)
    s = s + c_q[..., :, None] - c_k[..., None, :]
    s = jnp.where(q_pos[:, None] >= k_pos[None, :], s, -jnp.inf)
    p = jax.nn.softmax(s, axis=-1)
    return jnp.einsum('bhqk,bhkd->bhqd', p.astype(v.dtype), v)


def _fox_attention(q, k, v, log_f):
    bsz, L, H, dh = q.shape
    n_real = L - N_META
    nb = n_real // Q_BLOCK
    c = jnp.cumsum(log_f, axis=1).transpose(0, 2, 1)
    qh = q.transpose(0, 2, 1, 3)
    kh = k.transpose(0, 2, 1, 3)
    vh = v.transpose(0, 2, 1, 3)
    k_pos = jnp.arange(L)
    meta_out = _fox_attend(qh[:, :, :N_META], c[:, :, :N_META], k_pos[:N_META],
                           kh[:, :, :N_META], vh[:, :, :N_META], c[:, :, :N_META], k_pos[:N_META])
    q_blocks = qh[:, :, N_META:].reshape(bsz, H, nb, Q_BLOCK, dh).transpose(2, 0, 1, 3, 4)
    c_blocks = c[:, :, N_META:].reshape(bsz, H, nb, Q_BLOCK).transpose(2, 0, 1, 3)

    def body(args):
        q_blk, c_q, i = args
        q_pos = N_META + i * Q_BLOCK + jnp.arange(Q_BLOCK)
        return _fox_attend(q_blk, c_q, q_pos, kh, vh, c, k_pos)

    real_out = lax.map(body, (q_blocks, c_blocks, jnp.arange(nb)))
    real_out = real_out.transpose(1, 0, 3, 2, 4).reshape(bsz, n_real, H * dh)
    meta_out = meta_out.transpose(0, 2, 1, 3).reshape(bsz, N_META, H * dh)
    return jnp.concatenate([meta_out, real_out], axis=1)


def _fwd_setup_inputs(seed: int = 0) -> dict:
    key = jax.random.key(seed)
    ks = jax.random.split(key, 20)
    nrm = jax.random.normal
    f32 = jnp.float32
    return {
        'x': nrm(ks[0], (BATCH, SEQ, D_MODEL), f32),
        'meta_tokens': nrm(ks[1], (N_META, D_MODEL), f32),
        'mix_norm_g': 1.0 + 0.1 * nrm(ks[2], (DEPTH, D_MODEL), f32),
        'w_in': nrm(ks[3], (DEPTH, D_MODEL, N_IN), f32) * D_MODEL ** -0.5,
        'b_forget': jax.random.uniform(ks[4], (DEPTH, ATTN_HEADS), f32, 1.0, 5.0),
        'w_conf_dw': nrm(ks[5], (DEPTH, CONF_KERNEL, D_CONF), f32) * CONF_KERNEL ** -0.5,
        'b_conf_dw': 0.02 * nrm(ks[6], (DEPTH, D_CONF), f32),
        'conf_ln_g': 1.0 + 0.1 * nrm(ks[7], (DEPTH, D_CONF), f32),
        'conf_ln_b': 0.02 * nrm(ks[8], (DEPTH, D_CONF), f32),
        'w_conf_pw': nrm(ks[9], (DEPTH, D_CONF, D_CONF), f32) * D_CONF ** -0.5,
        'b_conf_pw': 0.02 * nrm(ks[10], (DEPTH, D_CONF), f32),
        'w_sc_conv': nrm(ks[11], (DEPTH, SC_KERNEL, D_SC), f32) * SC_KERNEL ** -0.5,
        'w_out': nrm(ks[12], (DEPTH, D_MIX, D_MODEL), f32) * D_MIX ** -0.5,
        'mlp_norm_g': 1.0 + 0.1 * nrm(ks[13], (DEPTH, D_MODEL), f32),
        'w_mlp1': nrm(ks[14], (DEPTH, D_MODEL, D_FF), f32) * D_MODEL ** -0.5,
        'w_mlp2': nrm(ks[15], (DEPTH, D_FF, D_MODEL), f32) * D_FF ** -0.5,
        'final_norm_g': 1.0 + 0.1 * nrm(ks[16], (D_MODEL,), f32),
    }


def _fwd_reference(x, meta_tokens, mix_norm_g, w_in, b_forget, w_conf_dw, b_conf_dw, conf_ln_g,
              conf_ln_b, w_conf_pw, b_conf_pw, w_sc_conv, w_out, mlp_norm_g, w_mlp1, w_mlp2,
              final_norm_g):
    bsz = x.shape[0]
    meta = jnp.broadcast_to(meta_tokens[None].astype(x.dtype), (bsz, N_META, D_MODEL))
    h = jnp.concatenate([meta, x], axis=1)
    L = h.shape[1]
    sizes = [D_ATTN, D_ATTN, D_ATTN, ATTN_HEADS, D_CONF, D_CONF, D_SC, D_SC, D_SC]
    splits = np.cumsum(sizes)[:-1].tolist()
    for l in range(DEPTH):
        hn = _rmsnorm(h, mix_norm_g[l])
        proj = hn @ w_in[l]
        q, k, v, f_logit, conf_a, conf_gate, sc_b, sc_c, sc_u = jnp.split(proj, splits, axis=-1)
        log_f = jax.nn.log_sigmoid(f_logit.astype(jnp.float32) + b_forget[l].astype(jnp.float32))
        attn = _fox_attention(q.reshape(bsz, L, ATTN_HEADS, HEAD_DIM),
                              k.reshape(bsz, L, ATTN_HEADS, HEAD_DIM),
                              v.reshape(bsz, L, ATTN_HEADS, HEAD_DIM), log_f)
        glu = conf_a * jax.nn.sigmoid(conf_gate)
        dw = _causal_dwconv(glu, w_conf_dw[l]) + b_conf_dw[l]
        conf = jax.nn.silu(_layernorm(dw, conf_ln_g[l], conf_ln_b[l])) @ w_conf_pw[l] + b_conf_pw[l]
        sc = sc_b * _causal_dwconv(sc_c * sc_u, w_sc_conv[l])
        h = h + jnp.concatenate([attn, conf, sc], axis=-1) @ w_out[l]
        hn = _rmsnorm(h, mlp_norm_g[l])
        h = h + jnp.square(jax.nn.relu(hn @ w_mlp1[l])) @ w_mlp2[l]
    out = _rmsnorm(h, final_norm_g)
    return out[:, N_META:]


import jax as _jax
import jax.numpy as _jnp

TWIN_FORMAT = 'train_step'
FWD_PARAMS = ['x', 'meta_tokens', 'mix_norm_g', 'w_in', 'b_forget', 'w_conf_dw', 'b_conf_dw', 'conf_ln_g', 'conf_ln_b', 'w_conf_pw', 'b_conf_pw', 'w_sc_conv', 'w_out', 'mlp_norm_g', 'w_mlp1', 'w_mlp2', 'final_norm_g']
TWIN_WEIGHTS = ['meta_tokens', 'mix_norm_g', 'w_in', 'b_forget', 'w_conf_dw', 'b_conf_dw', 'conf_ln_g', 'conf_ln_b', 'w_conf_pw', 'b_conf_pw', 'w_sc_conv', 'w_out', 'mlp_norm_g', 'w_mlp1', 'w_mlp2', 'final_norm_g']
TWIN_DIFF_INPUT = 'x'
TWIN_INPUTS = ['x', 'meta_tokens', 'mix_norm_g', 'w_in', 'b_forget', 'w_conf_dw', 'b_conf_dw', 'conf_ln_g', 'conf_ln_b', 'w_conf_pw', 'b_conf_pw', 'w_sc_conv', 'w_out', 'mlp_norm_g', 'w_mlp1', 'w_mlp2', 'final_norm_g', 'loss_target', 'm_meta_tokens', 'm_mix_norm_g', 'm_w_in', 'm_b_forget', 'm_w_conf_dw', 'm_b_conf_dw', 'm_conf_ln_g', 'm_conf_ln_b', 'm_w_conf_pw', 'm_b_conf_pw', 'm_w_sc_conv', 'm_w_out', 'm_mlp_norm_g', 'm_w_mlp1', 'm_w_mlp2', 'm_final_norm_g', 'v_meta_tokens', 'v_mix_norm_g', 'v_w_in', 'v_b_forget', 'v_w_conf_dw', 'v_b_conf_dw', 'v_conf_ln_g', 'v_conf_ln_b', 'v_w_conf_pw', 'v_b_conf_pw', 'v_w_sc_conv', 'v_w_out', 'v_mlp_norm_g', 'v_w_mlp1', 'v_w_mlp2', 'v_final_norm_g']
TWIN_OUTPUTS = ['loss', 'grad_x', 'grad_meta_tokens', 'grad_mix_norm_g', 'grad_w_in', 'grad_b_forget', 'grad_w_conf_dw', 'grad_b_conf_dw', 'grad_conf_ln_g', 'grad_conf_ln_b', 'grad_w_conf_pw', 'grad_b_conf_pw', 'grad_w_sc_conv', 'grad_w_out', 'grad_mlp_norm_g', 'grad_w_mlp1', 'grad_w_mlp2', 'grad_final_norm_g', 'delta_meta_tokens', 'delta_mix_norm_g', 'delta_w_in', 'delta_b_forget', 'delta_w_conf_dw', 'delta_b_conf_dw', 'delta_conf_ln_g', 'delta_conf_ln_b', 'delta_w_conf_pw', 'delta_b_conf_pw', 'delta_w_sc_conv', 'delta_w_out', 'delta_mlp_norm_g', 'delta_w_mlp1', 'delta_w_mlp2', 'delta_final_norm_g', 'new_m_meta_tokens', 'new_m_mix_norm_g', 'new_m_w_in', 'new_m_b_forget', 'new_m_w_conf_dw', 'new_m_b_conf_dw', 'new_m_conf_ln_g', 'new_m_conf_ln_b', 'new_m_w_conf_pw', 'new_m_b_conf_pw', 'new_m_w_sc_conv', 'new_m_w_out', 'new_m_mlp_norm_g', 'new_m_w_mlp1', 'new_m_w_mlp2', 'new_m_final_norm_g', 'new_v_meta_tokens', 'new_v_mix_norm_g', 'new_v_w_in', 'new_v_b_forget', 'new_v_w_conf_dw', 'new_v_b_conf_dw', 'new_v_conf_ln_g', 'new_v_conf_ln_b', 'new_v_w_conf_pw', 'new_v_b_conf_pw', 'new_v_w_sc_conv', 'new_v_w_out', 'new_v_mlp_norm_g', 'new_v_w_mlp1', 'new_v_w_mlp2', 'new_v_final_norm_g']
TWIN_LEAF_KINDS = {'loss': 'loss', 'grad_x': 'grad_x', 'grad_meta_tokens': 'grad_w', 'grad_mix_norm_g': 'grad_w', 'grad_w_in': 'grad_w', 'grad_b_forget': 'grad_w', 'grad_w_conf_dw': 'grad_w', 'grad_b_conf_dw': 'grad_w', 'grad_conf_ln_g': 'grad_w', 'grad_conf_ln_b': 'grad_w', 'grad_w_conf_pw': 'grad_w', 'grad_b_conf_pw': 'grad_w', 'grad_w_sc_conv': 'grad_w', 'grad_w_out': 'grad_w', 'grad_mlp_norm_g': 'grad_w', 'grad_w_mlp1': 'grad_w', 'grad_w_mlp2': 'grad_w', 'grad_final_norm_g': 'grad_w', 'delta_meta_tokens': 'delta_w', 'delta_mix_norm_g': 'delta_w', 'delta_w_in': 'delta_w', 'delta_b_forget': 'delta_w', 'delta_w_conf_dw': 'delta_w', 'delta_b_conf_dw': 'delta_w', 'delta_conf_ln_g': 'delta_w', 'delta_conf_ln_b': 'delta_w', 'delta_w_conf_pw': 'delta_w', 'delta_b_conf_pw': 'delta_w', 'delta_w_sc_conv': 'delta_w', 'delta_w_out': 'delta_w', 'delta_mlp_norm_g': 'delta_w', 'delta_w_mlp1': 'delta_w', 'delta_w_mlp2': 'delta_w', 'delta_final_norm_g': 'delta_w', 'new_m_meta_tokens': 'new_m', 'new_m_mix_norm_g': 'new_m', 'new_m_w_in': 'new_m', 'new_m_b_forget': 'new_m', 'new_m_w_conf_dw': 'new_m', 'new_m_b_conf_dw': 'new_m', 'new_m_conf_ln_g': 'new_m', 'new_m_conf_ln_b': 'new_m', 'new_m_w_conf_pw': 'new_m', 'new_m_b_conf_pw': 'new_m', 'new_m_w_sc_conv': 'new_m', 'new_m_w_out': 'new_m', 'new_m_mlp_norm_g': 'new_m', 'new_m_w_mlp1': 'new_m', 'new_m_w_mlp2': 'new_m', 'new_m_final_norm_g': 'new_m', 'new_v_meta_tokens': 'new_v', 'new_v_mix_norm_g': 'new_v', 'new_v_w_in': 'new_v', 'new_v_b_forget': 'new_v', 'new_v_w_conf_dw': 'new_v', 'new_v_b_conf_dw': 'new_v', 'new_v_conf_ln_g': 'new_v', 'new_v_conf_ln_b': 'new_v', 'new_v_w_conf_pw': 'new_v', 'new_v_b_conf_pw': 'new_v', 'new_v_w_sc_conv': 'new_v', 'new_v_w_out': 'new_v', 'new_v_mlp_norm_g': 'new_v', 'new_v_w_mlp1': 'new_v', 'new_v_w_mlp2': 'new_v', 'new_v_final_norm_g': 'new_v'}


def _forward(args):
    return _fwd_reference(*[args[k] for k in FWD_PARAMS])


def _output_shape():
    def fwd():
        inp = _fwd_setup_inputs(0)
        return _fwd_reference(*[inp[k] for k in FWD_PARAMS])
    out = _jax.eval_shape(fwd)
    return out.shape, out.dtype

N_MICROBATCH = 1
ADAM_LR = 0.001
ADAM_B1 = 0.9
ADAM_B2 = 0.999
ADAM_EPS = 1e-08
ADAM_WD = 0.01
ADAM_STEP = 10
PER_EXAMPLE_BATCH_AXIS = {'x': 0, 'loss_target': 0}
SHARED_INPUTS = []
_WEIGHT_DTYPES = {'meta_tokens': _jnp.float32, 'mix_norm_g': _jnp.float32, 'w_in': _jnp.float32, 'b_forget': _jnp.float32, 'w_conf_dw': _jnp.float32, 'b_conf_dw': _jnp.float32, 'conf_ln_g': _jnp.float32, 'conf_ln_b': _jnp.float32, 'w_conf_pw': _jnp.float32, 'b_conf_pw': _jnp.float32, 'w_sc_conv': _jnp.float32, 'w_out': _jnp.float32, 'mlp_norm_g': _jnp.float32, 'w_mlp1': _jnp.float32, 'w_mlp2': _jnp.float32, 'final_norm_g': _jnp.float32}
MOMENT_SCALE = {'meta_tokens': 8.712322e-03, 'mix_norm_g': 3.121754e-01, 'w_in': 1.903631e-01, 'b_forget': 2.892088e-01, 'w_conf_dw': 2.561329e-01, 'b_conf_dw': 2.552552e+00, 'conf_ln_g': 9.124006e-01, 'conf_ln_b': 1.398640e+00, 'w_conf_pw': 5.076665e-01, 'b_conf_pw': 2.643803e+00, 'w_sc_conv': 2.181140e-01, 'w_out': 3.654361e-01, 'mlp_norm_g': 3.247259e-01, 'w_mlp1': 1.789016e-01, 'w_mlp2': 1.123115e+00, 'final_norm_g': 6.571124e+01}


def _to_microbatches(a, axis):
    t = _jnp.moveaxis(a, axis, 0)
    t = t.reshape((N_MICROBATCH, t.shape[0] // N_MICROBATCH) + t.shape[1:])
    return _jnp.moveaxis(t, 1, axis + 1)


def setup_inputs(seed: int = 0) -> dict:
    inp = _fwd_setup_inputs(seed)
    key = _jax.random.fold_in(_jax.random.key(seed), 7919)
    shape, _ = _output_shape()
    out = dict(inp)
    out["loss_target"] = _jax.random.normal(_jax.random.fold_in(key, 0), shape, _jnp.float32)
    for i, name in enumerate(TWIN_WEIGHTS):
        w = inp[name].astype(_jnp.float32)
        if MOMENT_SCALE is None:
            s = _jnp.sqrt(_jnp.mean(_jnp.square(w)) + 1e-30)
        else:
            s = MOMENT_SCALE[name]
        km, kv = _jax.random.split(_jax.random.fold_in(key, i + 1))
        out[name] = w
        out["m_" + name] = s * _jax.random.normal(km, w.shape, _jnp.float32)
        out["v_" + name] = (s * s) * _jax.random.uniform(kv, w.shape, _jnp.float32, 0.5, 1.5)
    if N_MICROBATCH > 1:
        for name, axis in PER_EXAMPLE_BATCH_AXIS.items():
            out[name] = _to_microbatches(out[name], axis)
    return {'x': out['x'], 'meta_tokens': out['meta_tokens'], 'mix_norm_g': out['mix_norm_g'], 'w_in': out['w_in'], 'b_forget': out['b_forget'], 'w_conf_dw': out['w_conf_dw'], 'b_conf_dw': out['b_conf_dw'], 'conf_ln_g': out['conf_ln_g'], 'conf_ln_b': out['conf_ln_b'], 'w_conf_pw': out['w_conf_pw'], 'b_conf_pw': out['b_conf_pw'], 'w_sc_conv': out['w_sc_conv'], 'w_out': out['w_out'], 'mlp_norm_g': out['mlp_norm_g'], 'w_mlp1': out['w_mlp1'], 'w_mlp2': out['w_mlp2'], 'final_norm_g': out['final_norm_g'], 'loss_target': out['loss_target'], 'm_meta_tokens': out['m_meta_tokens'], 'm_mix_norm_g': out['m_mix_norm_g'], 'm_w_in': out['m_w_in'], 'm_b_forget': out['m_b_forget'], 'm_w_conf_dw': out['m_w_conf_dw'], 'm_b_conf_dw': out['m_b_conf_dw'], 'm_conf_ln_g': out['m_conf_ln_g'], 'm_conf_ln_b': out['m_conf_ln_b'], 'm_w_conf_pw': out['m_w_conf_pw'], 'm_b_conf_pw': out['m_b_conf_pw'], 'm_w_sc_conv': out['m_w_sc_conv'], 'm_w_out': out['m_w_out'], 'm_mlp_norm_g': out['m_mlp_norm_g'], 'm_w_mlp1': out['m_w_mlp1'], 'm_w_mlp2': out['m_w_mlp2'], 'm_final_norm_g': out['m_final_norm_g'], 'v_meta_tokens': out['v_meta_tokens'], 'v_mix_norm_g': out['v_mix_norm_g'], 'v_w_in': out['v_w_in'], 'v_b_forget': out['v_b_forget'], 'v_w_conf_dw': out['v_w_conf_dw'], 'v_b_conf_dw': out['v_b_conf_dw'], 'v_conf_ln_g': out['v_conf_ln_g'], 'v_conf_ln_b': out['v_conf_ln_b'], 'v_w_conf_pw': out['v_w_conf_pw'], 'v_b_conf_pw': out['v_b_conf_pw'], 'v_w_sc_conv': out['v_w_sc_conv'], 'v_w_out': out['v_w_out'], 'v_mlp_norm_g': out['v_mlp_norm_g'], 'v_w_mlp1': out['v_w_mlp1'], 'v_w_mlp2': out['v_w_mlp2'], 'v_final_norm_g': out['v_final_norm_g']}


def _loss(weights, diff, rest, loss_target):
    with _jax.named_scope("forward"):
        args = {**rest, TWIN_DIFF_INPUT: diff, **{k: w.astype(_WEIGHT_DTYPES[k]) for k, w in weights.items()}}
        y = _forward(args)
    with _jax.named_scope("loss_head"):
        err = _jnp.square(y.astype(_jnp.float32) - loss_target)
        return 0.5 * _jnp.sum(_jnp.mean(err, axis=-1)) if err.ndim else 0.5 * err


def _adamw(w, g, m, v):
    m = ADAM_B1 * m + (1.0 - ADAM_B1) * g
    v = ADAM_B2 * v + (1.0 - ADAM_B2) * _jnp.square(g)
    m_hat = m / (1.0 - ADAM_B1 ** ADAM_STEP)
    v_hat = v / (1.0 - ADAM_B2 ** ADAM_STEP)
    delta = -ADAM_LR * (m_hat / (_jnp.sqrt(v_hat) + ADAM_EPS) + ADAM_WD * w)
    return delta, m, v


def reference(x, meta_tokens, mix_norm_g, w_in, b_forget, w_conf_dw, b_conf_dw, conf_ln_g, conf_ln_b, w_conf_pw, b_conf_pw, w_sc_conv, w_out, mlp_norm_g, w_mlp1, w_mlp2, final_norm_g, loss_target, m_meta_tokens, m_mix_norm_g, m_w_in, m_b_forget, m_w_conf_dw, m_b_conf_dw, m_conf_ln_g, m_conf_ln_b, m_w_conf_pw, m_b_conf_pw, m_w_sc_conv, m_w_out, m_mlp_norm_g, m_w_mlp1, m_w_mlp2, m_final_norm_g, v_meta_tokens, v_mix_norm_g, v_w_in, v_b_forget, v_w_conf_dw, v_b_conf_dw, v_conf_ln_g, v_conf_ln_b, v_w_conf_pw, v_b_conf_pw, v_w_sc_conv, v_w_out, v_mlp_norm_g, v_w_mlp1, v_w_mlp2, v_final_norm_g):
    given = dict(x=x, meta_tokens=meta_tokens, mix_norm_g=mix_norm_g, w_in=w_in, b_forget=b_forget, w_conf_dw=w_conf_dw, b_conf_dw=b_conf_dw, conf_ln_g=conf_ln_g, conf_ln_b=conf_ln_b, w_conf_pw=w_conf_pw, b_conf_pw=b_conf_pw, w_sc_conv=w_sc_conv, w_out=w_out, mlp_norm_g=mlp_norm_g, w_mlp1=w_mlp1, w_mlp2=w_mlp2, final_norm_g=final_norm_g, loss_target=loss_target, m_meta_tokens=m_meta_tokens, m_mix_norm_g=m_mix_norm_g, m_w_in=m_w_in, m_b_forget=m_b_forget, m_w_conf_dw=m_w_conf_dw, m_b_conf_dw=m_b_conf_dw, m_conf_ln_g=m_conf_ln_g, m_conf_ln_b=m_conf_ln_b, m_w_conf_pw=m_w_conf_pw, m_b_conf_pw=m_b_conf_pw, m_w_sc_conv=m_w_sc_conv, m_w_out=m_w_out, m_mlp_norm_g=m_mlp_norm_g, m_w_mlp1=m_w_mlp1, m_w_mlp2=m_w_mlp2, m_final_norm_g=m_final_norm_g, v_meta_tokens=v_meta_tokens, v_mix_norm_g=v_mix_norm_g, v_w_in=v_w_in, v_b_forget=v_b_forget, v_w_conf_dw=v_w_conf_dw, v_b_conf_dw=v_b_conf_dw, v_conf_ln_g=v_conf_ln_g, v_conf_ln_b=v_conf_ln_b, v_w_conf_pw=v_w_conf_pw, v_b_conf_pw=v_b_conf_pw, v_w_sc_conv=v_w_sc_conv, v_w_out=v_w_out, v_mlp_norm_g=v_mlp_norm_g, v_w_mlp1=v_w_mlp1, v_w_mlp2=v_w_mlp2, v_final_norm_g=v_final_norm_g)
    weights = {n: given[n] for n in TWIN_WEIGHTS}
    shared = {n: given[n] for n in SHARED_INPUTS}
    per_example = {n: given[n] for n in ['x']}
    grad_fn = _jax.value_and_grad(_loss, argnums=(0, 1))

    def one_microbatch(ex, loss_target):
        ex = dict(ex)
        diff = ex.pop(TWIN_DIFF_INPUT)
        return grad_fn(weights, diff, {**shared, **ex}, loss_target)

    if N_MICROBATCH == 1:
        loss, (grad_w, grad_x) = one_microbatch(per_example, given["loss_target"])
    else:
        def body(carry, xs):
            loss_sum, grad_sum = carry
            l_k, (gw_k, gx_k) = one_microbatch(xs[0], xs[1])
            with _jax.named_scope("update"):
                return (loss_sum + l_k, _jax.tree.map(_jnp.add, grad_sum, gw_k)), gx_k

        init = (_jnp.zeros((), _jnp.float32), _jax.tree.map(_jnp.zeros_like, weights))
        (loss, grad_w), grad_x = _jax.lax.scan(body, init, (per_example, given["loss_target"]))
    with _jax.named_scope("update"):
        delta_w, new_m, new_v = {}, {}, {}
        for n in TWIN_WEIGHTS:
            delta_w[n], new_m[n], new_v[n] = _adamw(weights[n], grad_w[n], given["m_" + n], given["v_" + n])
    return (loss, grad_x, *[grad_w[n] for n in TWIN_WEIGHTS], *[delta_w[n] for n in TWIN_WEIGHTS],
            *[new_m[n] for n in TWIN_WEIGHTS], *[new_v[n] for n in TWIN_WEIGHTS])
```

```python
import jax
import jax.numpy as jnp
from jax import lax
from jax.experimental import pallas as pl
from jax.experimental.pallas import tpu as pltpu

F32, BF16 = jnp.float32, jnp.bfloat16
D_MODEL = 1024
D_ATTN = 512
D_CONF = 256
N_HEADS = 8
HEAD_DIM = 64
N_PAIRS = N_HEADS // 2
PAIR_W = 2 * HEAD_DIM
N_META = 16
D_FF = 4096
CONF_K = 31
SC_K = 3
HALO = 32
N_F_PAD = 128
N_REST = 5 * D_CONF + N_F_PAD
F_BLK = 5 * D_CONF // N_F_PAD
EPS = 1e-6
NEG = -1e30
BIG_TILE = 640
SMALL_TILE = 128
MLP_TILE_DIV = 2
V7X_VMEM_BYTES = 64 * 1024 * 1024
ADAM_LR, ADAM_B1, ADAM_B2, ADAM_EPS, ADAM_WD, ADAM_STEP = 0.001, 0.9, 0.999, 1e-08, 0.01, 10
MESH = pl.DeviceIdType.MESH
HIGHEST = lax.Precision.HIGHEST
SDS = jax.ShapeDtypeStruct


def _cparams(sem=None, vmem_mb=None):
    kw = {}
    if sem is not None:
        kw["dimension_semantics"] = sem
    if vmem_mb is not None:
        assert vmem_mb * 1024 * 1024 <= V7X_VMEM_BYTES
        kw["vmem_limit_bytes"] = vmem_mb * 1024 * 1024
    return pltpu.CompilerParams(**kw)


def _dot(a, b):
    return jnp.dot(a, b, preferred_element_type=F32)


def _dot_nt(a, b):
    return lax.dot_general(a, b, (((1,), (1,)), ((), ())), preferred_element_type=F32)


def _dot_tn(a, b):
    return lax.dot_general(a, b, (((0,), (0,)), ((), ())), preferred_element_type=F32)


def _sigmoid(x):
    return 1.0 / (1.0 + jnp.exp(-x))


def _rms_fwd(x, g):
    r = lax.rsqrt(jnp.mean(x * x, axis=-1, keepdims=True) + EPS)
    xn = x * r
    return r, xn, xn * g


def _rms_bwd(r, xn, g, dy):
    u = dy * g
    return r * (u - xn * jnp.mean(u * xn, axis=-1, keepdims=True))


def _colsum(x):
    return jnp.sum(x, axis=0, keepdims=True)


def _row_tile(lp):
    return BIG_TILE if lp % BIG_TILE == 0 else SMALL_TILE


def _padded_len(l):
    t = BIG_TILE if l >= BIG_TILE else SMALL_TILE
    return -(-l // t) * t


def _chip_exchange(arrs, gather, name):
    n = len(arrs)

    def body(*refs):
        ins, outs = refs[:n], refs[n:2 * n]
        send_sems, recv_sems, local_sems = refs[2 * n:]
        x, y, c = lax.axis_index("x"), lax.axis_index("y"), lax.axis_index("c")
        me = 2 * x + y
        chips = [(1 - x, y), (x, 1 - y), (1 - x, 1 - y)]
        local, sends, recvs = [], [], []
        for i in range(n):
            src_me = ins[i] if gather else ins[i].at[me]
            local.append(pltpu.make_async_copy(src_me, outs[i].at[me], local_sems.at[i]))
            for j, (px, py) in enumerate(chips):
                k = 3 * i + j
                peer = 2 * px + py
                src = ins[i] if gather else ins[i].at[peer]
                sends.append(pltpu.make_async_remote_copy(
                    src_ref=src, dst_ref=outs[i].at[me], send_sem=send_sems.at[k], recv_sem=recv_sems.at[k],
                    device_id=(px, py, c), device_id_type=MESH))
                recvs.append(pltpu.make_async_remote_copy(
                    src_ref=src, dst_ref=outs[i].at[peer], send_sem=send_sems.at[k], recv_sem=recv_sems.at[k],
                    device_id=(px, py, c), device_id_type=MESH))
        for cp in local + sends:
            cp.start()
        for cp in recvs:
            cp.wait_recv()
        for cp in sends:
            cp.wait_send()
        for cp in local:
            cp.wait()

    any_spec = pl.BlockSpec(memory_space=pl.ANY)
    out_shape = [SDS((4,) + a.shape, a.dtype) if gather else SDS(a.shape, a.dtype) for a in arrs]
    return pl.pallas_call(
        body, name=name, out_shape=out_shape,
        in_specs=[any_spec] * n, out_specs=[any_spec] * n,
        scratch_shapes=[pltpu.SemaphoreType.DMA((3 * n,)), pltpu.SemaphoreType.DMA((3 * n,)),
                        pltpu.SemaphoreType.DMA((n,))],
    )(*arrs)


def _sibling_exchange(arrs, name):
    n = len(arrs)

    def body(*refs):
        ins, outs = refs[:n], refs[n:2 * n]
        send_sems, recv_sems = refs[2 * n:]
        sib = (lax.axis_index("x"), lax.axis_index("y"), 1 - lax.axis_index("c"))
        cps = [pltpu.make_async_remote_copy(src_ref=ins[i], dst_ref=outs[i], send_sem=send_sems.at[i],
                                            recv_sem=recv_sems.at[i], device_id=sib, device_id_type=MESH)
               for i in range(n)]
        for cp in cps:
            cp.start()
        for cp in cps:
            cp.wait_recv()
        for cp in cps:
            cp.wait_send()

    any_spec = pl.BlockSpec(memory_space=pl.ANY)
    return pl.pallas_call(
        body, name=name, out_shape=[SDS(a.shape, a.dtype) for a in arrs],
        in_specs=[any_spec] * n, out_specs=[any_spec] * n,
        scratch_shapes=[pltpu.SemaphoreType.DMA((n,)), pltpu.SemaphoreType.DMA((n,))],
    )(*arrs)


def _proj_fwd(h, g, wqkv, wrest, name):
    lp = h.shape[0]
    tm = _row_tile(lp)

    def body(h_ref, g_ref, wq_ref, wr_ref, hn_ref, qkv_ref, rest_ref):
        _, _, y = _rms_fwd(h_ref[...], g_ref[...])
        hn = y.astype(BF16)
        hn_ref[...] = hn
        qkv = _dot(hn, wq_ref[...])
        qkv_ref[:, :D_ATTN] = (qkv[:, :D_ATTN] * (HEAD_DIM ** -0.5)).astype(BF16)
        qkv_ref[:, D_ATTN:] = qkv[:, D_ATTN:].astype(BF16)
        rest_ref[...] = _dot(hn, wr_ref[...])

    row = lambda w: pl.BlockSpec((tm, w), lambda i: (i, 0))
    full = lambda a: pl.BlockSpec(a.shape, lambda i: (0, 0))
    return pl.pallas_call(
        body, name=name, grid=(lp // tm,),
        in_specs=[row(D_MODEL), full(g), full(wqkv), full(wrest)],
        out_specs=[row(D_MODEL), row(3 * D_ATTN), row(N_REST)],
        out_shape=[SDS((lp, D_MODEL), BF16), SDS((lp, 3 * D_ATTN), BF16), SDS((lp, N_REST), F32)],
        compiler_params=_cparams(("parallel",), 48),
    )(h, g, wqkv, wrest)


def _mm_tn(a, b, name):
    kk, m = a.shape
    n = b.shape[1]
    tk = _row_tile(kk)
    tmo = min(m, 1024)
    tno = n if n <= 1536 else 1024

    def body(a_ref, b_ref, o_ref):
        @pl.when(pl.program_id(2) == 0)
        def _():
            o_ref[...] = jnp.zeros_like(o_ref)
        o_ref[...] += _dot_tn(a_ref[...], b_ref[...].astype(BF16))

    return pl.pallas_call(
        body, name=name, grid=(m // tmo, n // tno, kk // tk),
        in_specs=[pl.BlockSpec((tk, tmo), lambda i, j, k: (k, i)), pl.BlockSpec((tk, tno), lambda i, j, k: (k, j))],
        out_specs=pl.BlockSpec((tmo, tno), lambda i, j, k: (i, j)),
        out_shape=SDS((m, n), F32),
        compiler_params=_cparams(("parallel", "parallel", "arbitrary"), 48),
    )(a, b)


def _mix_out(h, attn, cs, wa, wc, name):
    lp = h.shape[0]
    tm = _row_tile(lp)

    def body(h_ref, at_ref, cs_ref, wa_ref, wc_ref, o_ref):
        o_ref[...] = h_ref[...] + _dot(at_ref[...], wa_ref[...]) + _dot(cs_ref[...], wc_ref[...])

    row = lambda w: pl.BlockSpec((tm, w), lambda i: (i, 0))
    full = lambda a: pl.BlockSpec(a.shape, lambda i: (0, 0))
    return pl.pallas_call(
        body, name=name, grid=(lp // tm,),
        in_specs=[row(D_MODEL), row(D_ATTN), row(D_ATTN), full(wa), full(wc)],
        out_specs=row(D_MODEL), out_shape=SDS((lp, D_MODEL), F32),
        compiler_params=_cparams(("parallel",), 48),
    )(h, attn, cs, wa, wc)


def _mlp_fwd(h, g, w1, w2, name):
    lp = h.shape[0]
    tm = _row_tile(lp) // MLP_TILE_DIV

    def body(h_ref, g_ref, w1_ref, w2_ref, hn_ref, a_ref, r_ref, o_ref):
        x = h_ref[...]
        _, _, y = _rms_fwd(x, g_ref[...])
        hn = y.astype(BF16)
        hn_ref[...] = hn
        a = _dot(hn, w1_ref[...])
        a_ref[...] = a
        r = jnp.square(jnp.maximum(a, 0.0)).astype(BF16)
        r_ref[...] = r
        o_ref[...] = x + _dot(r, w2_ref[...])

    row = lambda w: pl.BlockSpec((tm, w), lambda i: (i, 0))
    vmem = pl.BlockSpec(memory_space=pltpu.VMEM)
    return pl.pallas_call(
        body, name=name, grid=(lp // tm,),
        in_specs=[row(D_MODEL), pl.BlockSpec(g.shape, lambda i: (0, 0)), vmem, vmem],
        out_specs=[row(D_MODEL), row(D_FF), row(D_FF), row(D_MODEL)],
        out_shape=[SDS((lp, D_MODEL), BF16), SDS((lp, D_FF), F32), SDS((lp, D_FF), BF16), SDS((lp, D_MODEL), F32)],
        compiler_params=_cparams(("parallel",), 60),
    )(h, g, w1, w2)


def _mlp_bwd(dh2, h1, g, a, w1, w2, name):
    lp = h1.shape[0]
    tm = _row_tile(lp) // MLP_TILE_DIV

    def body(dy_ref, h_ref, g_ref, a_ref, w1_ref, w2_ref, da_ref, dh_ref, dg_ref):
        @pl.when(pl.program_id(0) == 0)
        def _():
            dg_ref[...] = jnp.zeros_like(dg_ref)
        dy = dy_ref[...]
        dr = _dot_nt(dy.astype(BF16), w2_ref[...])
        da = (dr * (2.0 * jnp.maximum(a_ref[...], 0.0))).astype(BF16)
        da_ref[...] = da
        dhn = _dot_nt(da, w1_ref[...])
        r, xn, _ = _rms_fwd(h_ref[...], g_ref[...])
        dg_ref[...] += _colsum(dhn * xn)
        dh_ref[...] = dy + _rms_bwd(r, xn, g_ref[...], dhn)

    row = lambda w: pl.BlockSpec((tm, w), lambda i: (i, 0))
    vmem = pl.BlockSpec(memory_space=pltpu.VMEM)
    vec = pl.BlockSpec((1, D_MODEL), lambda i: (0, 0))
    return pl.pallas_call(
        body, name=name, grid=(lp // tm,),
        in_specs=[row(D_MODEL), row(D_MODEL), vec, row(D_FF), vmem, vmem],
        out_specs=[row(D_FF), row(D_MODEL), vec],
        out_shape=[SDS((lp, D_FF), BF16), SDS((lp, D_MODEL), F32), SDS((1, D_MODEL), F32)],
        compiler_params=_cparams(("arbitrary",), 60),
    )(dh2, h1, g, a, w1, w2)


def _mix_bwd(dh1, wa, wc, name):
    lp = dh1.shape[0]
    tm = _row_tile(lp)

    def body(dy_ref, wa_ref, wc_ref, do_ref, dcs_ref):
        dy = dy_ref[...].astype(BF16)
        do_ref[...] = _dot_nt(dy, wa_ref[...]).astype(BF16)
        dcs_ref[...] = _dot_nt(dy, wc_ref[...])

    row = lambda w: pl.BlockSpec((tm, w), lambda i: (i, 0))
    full = lambda x: pl.BlockSpec(x.shape, lambda i: (0, 0))
    return pl.pallas_call(
        body, name=name, grid=(lp // tm,),
        in_specs=[row(D_MODEL), full(wa), full(wc)],
        out_specs=[row(D_ATTN), row(D_ATTN)],
        out_shape=[SDS((lp, D_ATTN), BF16), SDS((lp, D_ATTN), F32)],
        compiler_params=_cparams(("parallel",), 48),
    )(dh1, wa, wc)


def _in_bwd(dh1, h, g, dq, dk, dv, drest, wqkv, wrest, name):
    lp = h.shape[0]
    tm = _row_tile(lp)

    def body(dy_ref, h_ref, g_ref, dq_ref, dk_ref, dv_ref, dr_ref, wq_ref, wr_ref, dh_ref, dg_ref):
        @pl.when(pl.program_id(0) == 0)
        def _():
            dg_ref[...] = jnp.zeros_like(dg_ref)
        dhn = (_dot_nt(dq_ref[...], wq_ref[:, 0:D_ATTN]) + _dot_nt(dk_ref[...], wq_ref[:, D_ATTN:2 * D_ATTN])
               + _dot_nt(dv_ref[...], wq_ref[:, 2 * D_ATTN:]) + _dot_nt(dr_ref[...], wr_ref[...]))
        r, xn, _ = _rms_fwd(h_ref[...], g_ref[...])
        dg_ref[...] += _colsum(dhn * xn)
        dh_ref[...] = dy_ref[...] + _rms_bwd(r, xn, g_ref[...], dhn)

    row = lambda w: pl.BlockSpec((tm, w), lambda i: (i, 0))
    full = lambda x: pl.BlockSpec(x.shape, lambda i: (0, 0))
    vec = pl.BlockSpec((1, D_MODEL), lambda i: (0, 0))
    return pl.pallas_call(
        body, name=name, grid=(lp // tm,),
        in_specs=[row(D_MODEL), row(D_MODEL), vec, row(D_ATTN), row(D_ATTN), row(D_ATTN), row(N_REST),
                  full(wqkv), full(wrest)],
        out_specs=[row(D_MODEL), vec],
        out_shape=[SDS((lp, D_MODEL), F32), SDS((1, D_MODEL), F32)],
        compiler_params=_cparams(("arbitrary",), 48),
    )(dh1, h, g, dq, dk, dv, drest, wqkv, wrest)


def _loss_head(h, target, g, seq, name):
    lp = h.shape[0]
    tm = _row_tile(lp)

    def body(h_ref, t_ref, g_ref, dh_ref, loss_ref, dg_ref):
        i = pl.program_id(0)

        @pl.when(i == 0)
        def _():
            dg_ref[...] = jnp.zeros_like(dg_ref)
            loss_ref[...] = jnp.zeros_like(loss_ref)
        r, xn, y = _rms_fwd(h_ref[...], g_ref[...])
        rows = i * tm + lax.broadcasted_iota(jnp.int32, (tm, 1), 0)
        real = (rows >= N_META) & (rows < N_META + seq)
        e = jnp.where(real, y - t_ref[...], 0.0)
        loss_ref[...] += 0.5 * jnp.sum(jnp.mean(e * e, axis=-1, keepdims=True), axis=0, keepdims=True)
        dy = e * (1.0 / D_MODEL)
        dg_ref[...] += _colsum(dy * xn)
        dh_ref[...] = _rms_bwd(r, xn, g_ref[...], dy)

    row = pl.BlockSpec((tm, D_MODEL), lambda i: (i, 0))
    vec = pl.BlockSpec((1, D_MODEL), lambda i: (0, 0))
    return pl.pallas_call(
        body, name=name, grid=(lp // tm,),
        in_specs=[row, row, vec],
        out_specs=[row, pl.BlockSpec((1, 1), lambda i: (0, 0)), vec],
        out_shape=[SDS((lp, D_MODEL), F32), SDS((1, 1), F32), SDS((1, D_MODEL), F32)],
        compiler_params=_cparams(("arbitrary",), 48),
    )(h, target, g)


def _shift_down(xcat, j):
    return (xcat if j == 0 else pltpu.roll(xcat, j, 0))[HALO:, :]


def _shift_up(xcat, j, tm):
    n = xcat.shape[0]
    return (xcat if j == 0 else pltpu.roll(xcat, n - j, 0))[:tm, :]


def _layernorm_parts(x):
    mu = jnp.mean(x, axis=-1, keepdims=True)
    xc = x - mu
    rstd = lax.rsqrt(jnp.mean(xc * xc, axis=-1, keepdims=True) + EPS)
    return rstd, xc * rstd


def _log_sigmoid(z):
    return jnp.minimum(z, 0.0) - jnp.log(1.0 + jnp.exp(-jnp.abs(z)))


def _conv_fwd(rest, wdw, vec, wpw, wsc, name):
    lp = rest.shape[0]
    tm = _row_tile(lp)
    per = tm // HALO

    def body(a_ref, g_ref, ah_ref, gh_ref, b_ref, c_ref, u_ref, ch_ref, uh_ref, f_ref,
             wdw_ref, vec_ref, wpw_ref, wsc_ref, cs_ref, dwc_ref, cv_ref, cc_ref, carry):
        i = pl.program_id(0)
        first = i == 0

        @pl.when(first)
        def _():
            carry[...] = jnp.zeros_like(carry)
        glu = a_ref[...] * _sigmoid(g_ref[...])
        glu_h = jnp.where(first, 0.0, ah_ref[...] * _sigmoid(gh_ref[...]))
        xcat = jnp.concatenate([glu_h, glu], axis=0)
        acc = jnp.zeros((tm, D_CONF), F32) + vec_ref[0:1, :]
        for k in range(CONF_K):
            acc = acc + wdw_ref[k:k + 1, :] * _shift_down(xcat, CONF_K - 1 - k)
        dwc_ref[...] = acc
        _, y = _layernorm_parts(acc)
        ln = y * vec_ref[1:2, :] + vec_ref[2:3, :]
        sw = ln * _sigmoid(ln)
        conf = _dot(sw.astype(BF16), wpw_ref[...]) + vec_ref[3:4, :]
        p = c_ref[...] * u_ref[...]
        p_h = jnp.where(first, 0.0, ch_ref[...] * uh_ref[...])
        pcat = jnp.concatenate([p_h, p], axis=0)
        cv = jnp.zeros((tm, D_CONF), F32)
        for k in range(SC_K):
            cv = cv + wsc_ref[k:k + 1, :] * _shift_down(pcat, SC_K - 1 - k)
        cv_ref[...] = cv
        cs_ref[:, :D_CONF] = conf.astype(BF16)
        cs_ref[:, D_CONF:] = (b_ref[...] * cv).astype(BF16)
        logf = _log_sigmoid(f_ref[...] + vec_ref[4:5, :N_F_PAD])
        tri = (lax.broadcasted_iota(jnp.int32, (tm, tm), 0) >= lax.broadcasted_iota(jnp.int32, (tm, tm), 1))
        c = jnp.dot(tri.astype(F32), logf, precision=HIGHEST, preferred_element_type=F32) + carry[0:1, :]
        cc_ref[...] = c
        carry[0:1, :] = c[tm - 1:tm, :]

    cur = lambda j: pl.BlockSpec((tm, D_CONF), lambda i: (i, j))
    prev = lambda j: pl.BlockSpec((HALO, D_CONF), lambda i: (jnp.maximum(i * per - 1, 0), j))
    full = lambda x: pl.BlockSpec(x.shape, lambda i: (0, 0))
    return pl.pallas_call(
        body, name=name, grid=(lp // tm,),
        in_specs=[cur(0), cur(1), prev(0), prev(1), cur(2), cur(3), cur(4), prev(3), prev(4),
                  pl.BlockSpec((tm, N_F_PAD), lambda i: (i, F_BLK)),
                  full(wdw), full(vec), full(wpw), full(wsc)],
        out_specs=[pl.BlockSpec((tm, 2 * D_CONF), lambda i: (i, 0)), cur(0), cur(0),
                   pl.BlockSpec((tm, N_F_PAD), lambda i: (i, 0))],
        out_shape=[SDS((lp, 2 * D_CONF), BF16), SDS((lp, D_CONF), F32), SDS((lp, D_CONF), F32),
                   SDS((lp, N_F_PAD), F32)],
        scratch_shapes=[pltpu.VMEM((8, N_F_PAD), F32)],
        compiler_params=_cparams(("arbitrary",), 48),
    )(rest, rest, rest, rest, rest, rest, rest, rest, rest, rest, wdw, vec, wpw, wsc)


def _conv_bwd(dcs, dwc, cv, rest, dccol, wdw, vec, wpw, wsc, name):
    lp = rest.shape[0]
    tm = _row_tile(lp)
    per = tm // HALO
    nt = lp // tm
    n_halo_blocks = lp // HALO

    def body(dcf_ref, dsc_ref, dcfn_ref, dscn_ref, dwc_ref, dwcn_ref, cv_ref,
             a_ref, g_ref, ah_ref, gh_ref, b_ref, bn_ref, c_ref, u_ref, ch_ref, uh_ref, f_ref, dcc_ref,
             wdw_ref, vec_ref, wpw_ref, wsc_ref,
             dr_ref, dwdw_ref, dwsc_ref, dwpw_ref, dvec_ref, carry):
        i = pl.program_id(0)
        first = i == nt - 1
        last = i == 0

        @pl.when(i == 0)
        def _():
            carry[...] = jnp.zeros_like(carry)
            dwdw_ref[...] = jnp.zeros_like(dwdw_ref)
            dwsc_ref[...] = jnp.zeros_like(dwsc_ref)
            dwpw_ref[...] = jnp.zeros_like(dwpw_ref)
            dvec_ref[...] = jnp.zeros_like(dvec_ref)

        ln_g, ln_b = vec_ref[1:2, :], vec_ref[2:3, :]

        def ln_bwd(dconf, x):
            rstd, y = _layernorm_parts(x)
            ln = y * ln_g + ln_b
            sg = _sigmoid(ln)
            dsw = _dot_nt(dconf.astype(BF16), wpw_ref[...])
            dln = dsw * (sg * (1.0 + ln * (1.0 - sg)))
            dy = dln * ln_g
            ddw = rstd * (dy - jnp.mean(dy, axis=-1, keepdims=True) - y * jnp.mean(dy * y, axis=-1, keepdims=True))
            return ln * sg, y, dln, ddw

        dconf = dcf_ref[...]
        sw, y, dln, ddw = ln_bwd(dconf, dwc_ref[...])
        _, _, _, ddw_n = ln_bwd(dcfn_ref[...], dwcn_ref[...])
        ddw_n = jnp.where(last, 0.0, ddw_n)
        dvec_ref[0:1, :] += _colsum(ddw)
        dvec_ref[1:2, :] += _colsum(dln * y)
        dvec_ref[2:3, :] += _colsum(dln)
        dvec_ref[3:4, :] += _colsum(dconf)
        dwpw_ref[...] += _dot_tn(sw.astype(BF16), dconf.astype(BF16))

        a, sg = a_ref[...], _sigmoid(g_ref[...])
        glu = a * sg
        glu_h = jnp.where(first, 0.0, ah_ref[...] * _sigmoid(gh_ref[...]))
        xcat = jnp.concatenate([glu_h, glu], axis=0)
        dcat = jnp.concatenate([ddw, ddw_n], axis=0)
        dglu = jnp.zeros((tm, D_CONF), F32)
        for k in range(CONF_K):
            j = CONF_K - 1 - k
            dwdw_ref[k:k + 1, :] += _colsum(ddw * _shift_down(xcat, j))
            dglu = dglu + wdw_ref[k:k + 1, :] * _shift_up(dcat, j, tm)
        dr_ref[:, 0:D_CONF] = (dglu * sg).astype(BF16)
        dr_ref[:, D_CONF:2 * D_CONF] = (dglu * a * sg * (1.0 - sg)).astype(BF16)

        dsc = dsc_ref[...]
        b, c, u = b_ref[...], c_ref[...], u_ref[...]
        dcv = dsc * b
        dcv_n = jnp.where(last, 0.0, dscn_ref[...] * bn_ref[...])
        p_h = jnp.where(first, 0.0, ch_ref[...] * uh_ref[...])
        pcat = jnp.concatenate([p_h, c * u], axis=0)
        dccat = jnp.concatenate([dcv, dcv_n], axis=0)
        dp = jnp.zeros((tm, D_CONF), F32)
        for k in range(SC_K):
            j = SC_K - 1 - k
            dwsc_ref[k:k + 1, :] += _colsum(dcv * _shift_down(pcat, j))
            dp = dp + wsc_ref[k:k + 1, :] * _shift_up(dccat, j, tm)
        dr_ref[:, 2 * D_CONF:3 * D_CONF] = (dsc * cv_ref[...]).astype(BF16)
        dr_ref[:, 3 * D_CONF:4 * D_CONF] = (dp * u).astype(BF16)
        dr_ref[:, 4 * D_CONF:5 * D_CONF] = (dp * c).astype(BF16)

        tri = (lax.broadcasted_iota(jnp.int32, (tm, tm), 0) <= lax.broadcasted_iota(jnp.int32, (tm, tm), 1))
        dlogf = jnp.dot(tri.astype(F32), dcc_ref[...], precision=HIGHEST, preferred_element_type=F32) + carry[0:1, :]
        carry[0:1, :] = dlogf[0:1, :]
        dz = dlogf * _sigmoid(-(f_ref[...] + vec_ref[4:5, :N_F_PAD]))
        dvec_ref[4:5, 0:N_F_PAD] += _colsum(dz)
        dr_ref[:, 5 * D_CONF:] = dz.astype(BF16)

    rt = lambda i: nt - 1 - i
    cur = lambda j: pl.BlockSpec((tm, D_CONF), lambda i: (rt(i), j))
    prev = lambda j: pl.BlockSpec((HALO, D_CONF), lambda i: (jnp.maximum(rt(i) * per - 1, 0), j))
    nxt = lambda j: pl.BlockSpec((HALO, D_CONF), lambda i: (jnp.minimum((rt(i) + 1) * per, n_halo_blocks - 1), j))
    full = lambda x: pl.BlockSpec(x.shape, lambda i: (0, 0))
    acc = lambda r, w: pl.BlockSpec((r, w), lambda i: (0, 0))
    return pl.pallas_call(
        body, name=name, grid=(nt,),
        in_specs=[cur(0), cur(1), nxt(0), nxt(1), cur(0), nxt(0), cur(0),
                  cur(0), cur(1), prev(0), prev(1), cur(2), nxt(2), cur(3), cur(4), prev(3), prev(4),
                  pl.BlockSpec((tm, N_F_PAD), lambda i: (rt(i), F_BLK)),
                  pl.BlockSpec((tm, N_F_PAD), lambda i: (rt(i), 0)),
                  full(wdw), full(vec), full(wpw), full(wsc)],
        out_specs=[pl.BlockSpec((tm, N_REST), lambda i: (rt(i), 0)), acc(32, D_CONF), acc(8, D_CONF),
                   acc(D_CONF, D_CONF), acc(8, D_CONF)],
        out_shape=[SDS((lp, N_REST), BF16), SDS((32, D_CONF), F32), SDS((8, D_CONF), F32),
                   SDS((D_CONF, D_CONF), F32), SDS((8, D_CONF), F32)],
        scratch_shapes=[pltpu.VMEM((8, N_F_PAD), F32)],
        compiler_params=_cparams(("arbitrary",), 48),
    )(dcs, dcs, dcs, dcs, dwc, dwc, cv, rest, rest, rest, rest, rest, rest, rest, rest, rest, rest, rest, dccol,
      wdw, vec, wpw, wsc)


def _attn_fwd(qkv, crow, cpair, name):
    lp = qkv.shape[0]
    t = _row_tile(lp)
    nq = lp // t

    def body(q_ref, k_ref, v_ref, c_ref, cq_ref, o_ref, lse_ref, m_scr, l_scr, acc_scr):
        qi, ki = pl.program_id(1), pl.program_id(2)

        @pl.when(ki == 0)
        def _():
            m_scr[...] = jnp.full_like(m_scr, NEG)
            l_scr[...] = jnp.ones_like(l_scr)
            l_scr[:, 0:2] = jnp.zeros((t, 2), F32)
            acc_scr[...] = jnp.zeros_like(acc_scr)

        def step(diagonal):
            q, k, v = q_ref[...], k_ref[...], v_ref[...]
            lo = lax.broadcasted_iota(jnp.int32, (t, PAIR_W), 1) < HEAD_DIM
            pv, alpha = [], []
            for j in range(2):
                sel = lo if j == 0 else jnp.logical_not(lo)
                s = _dot_nt(jnp.where(sel, q, jnp.zeros_like(q)), k) + (cq_ref[:, j:j + 1] - c_ref[j:j + 1, :])
                if diagonal:
                    causal = (lax.broadcasted_iota(jnp.int32, (t, t), 0) >= lax.broadcasted_iota(jnp.int32, (t, t), 1))
                    s = jnp.where(causal, s, -jnp.inf)
                m_prev = m_scr[:, j:j + 1]
                m_new = jnp.maximum(m_prev, jnp.max(s, axis=1, keepdims=True))
                al = jnp.exp(m_prev - m_new)
                p = jnp.exp(s - m_new)
                l_scr[:, j:j + 1] = al * l_scr[:, j:j + 1] + jnp.sum(p, axis=1, keepdims=True)
                m_scr[:, j:j + 1] = m_new
                pv.append(_dot(p.astype(BF16), v))
                alpha.append(al)
            acc_scr[...] = acc_scr[...] * jnp.where(lo, alpha[0], alpha[1]) + jnp.where(lo, pv[0], pv[1])

        @pl.when(ki < qi)
        def _():
            step(False)

        @pl.when(ki == qi)
        def _():
            step(True)
            lo = lax.broadcasted_iota(jnp.int32, (t, PAIR_W), 1) < HEAD_DIM
            inv = jnp.where(lo, 1.0 / l_scr[:, 0:1], 1.0 / l_scr[:, 1:2])
            o_ref[...] = (acc_scr[...] * inv).astype(BF16)
            lse_ref[...] = m_scr[...] + jnp.log(l_scr[...])

    kv = lambda off: pl.BlockSpec((t, PAIR_W), lambda p, qi, ki: (jnp.minimum(ki, qi), off + p))
    qblk = pl.BlockSpec((t, PAIR_W), lambda p, qi, ki: (qi, p))
    return pl.pallas_call(
        body, name=name, grid=(N_PAIRS, nq, nq),
        in_specs=[qblk, kv(N_PAIRS), kv(2 * N_PAIRS),
                  pl.BlockSpec((None, 2, t), lambda p, qi, ki: (p, 0, jnp.minimum(ki, qi))), qblk],
        out_specs=[qblk, qblk],
        out_shape=[SDS((lp, D_ATTN), BF16), SDS((lp, D_ATTN), F32)],
        scratch_shapes=[pltpu.VMEM((t, PAIR_W), F32), pltpu.VMEM((t, PAIR_W), F32), pltpu.VMEM((t, PAIR_W), F32)],
        compiler_params=_cparams(("parallel", "parallel", "arbitrary"), 48),
    )(qkv, qkv, qkv, crow, cpair)


def _head_probs(qj, k, cq, ck, lse, diagonal):
    t = qj.shape[0]
    p = jnp.exp(_dot_nt(qj, k) + (cq - ck) - lse)
    if diagonal:
        causal = (lax.broadcasted_iota(jnp.int32, (t, t), 0) >= lax.broadcasted_iota(jnp.int32, (t, t), 1))
        p = jnp.where(causal, p, 0.0)
    return p


def _attn_delta(qkv, do, crow, cpair, lse, name):
    lp = qkv.shape[0]
    t = _row_tile(lp)
    nq = lp // t

    def body(q_ref, k_ref, v_ref, do_ref, c_ref, cq_ref, lse_ref, dl_ref, d_scr):
        qi, ki = pl.program_id(1), pl.program_id(2)

        @pl.when(ki == 0)
        def _():
            d_scr[...] = jnp.zeros_like(d_scr)

        def step(diagonal):
            q, k, v, dout = q_ref[...], k_ref[...], v_ref[...], do_ref[...]
            lo = lax.broadcasted_iota(jnp.int32, (t, PAIR_W), 1) < HEAD_DIM
            zero = jnp.zeros_like(q)
            for j in range(2):
                sel = lo if j == 0 else jnp.logical_not(lo)
                p = _head_probs(jnp.where(sel, q, zero), k, cq_ref[:, j:j + 1], c_ref[j:j + 1, :],
                                lse_ref[:, j:j + 1], diagonal)
                dp = _dot_nt(jnp.where(sel, dout, zero), v)
                d_scr[:, j:j + 1] += jnp.sum(p * dp, axis=1, keepdims=True)

        @pl.when(ki < qi)
        def _():
            step(False)

        @pl.when(ki == qi)
        def _():
            step(True)
            dl_ref[...] = d_scr[...]

    kv = lambda off: pl.BlockSpec((t, PAIR_W), lambda p, qi, ki: (jnp.minimum(ki, qi), off + p))
    qblk = pl.BlockSpec((t, PAIR_W), lambda p, qi, ki: (qi, p))
    return pl.pallas_call(
        body, name=name, grid=(N_PAIRS, nq, nq),
        in_specs=[qblk, kv(N_PAIRS), kv(2 * N_PAIRS), qblk,
                  pl.BlockSpec((None, 2, t), lambda p, qi, ki: (p, 0, jnp.minimum(ki, qi))), qblk, qblk],
        out_specs=qblk,
        out_shape=SDS((lp, D_ATTN), F32),
        scratch_shapes=[pltpu.VMEM((t, PAIR_W), F32)],
        compiler_params=_cparams(("parallel", "parallel", "arbitrary"), 48),
    )(qkv, qkv, qkv, do, crow, cpair, lse)


def _attn_bwd(qkv, do, crow, cpair, lse, delta, name):
    lp = qkv.shape[0]
    t = _row_tile(lp)
    nq = lp // t

    def body(q_ref, k_ref, v_ref, do_ref, c_ref, cq_ref, lse_ref, dl_ref, dq_ref, dk_ref, dv_ref, dc_ref,
             dq_scr, dk_scr, dv_scr, dc_scr):
        ki, qi = pl.program_id(1), pl.program_id(2)

        @pl.when((ki == 0) & (qi == 0))
        def _():
            dq_scr[...] = jnp.zeros_like(dq_scr)

        @pl.when(qi == 0)
        def _():
            dk_scr[...] = jnp.zeros_like(dk_scr)
            dv_scr[...] = jnp.zeros_like(dv_scr)
            dc_scr[...] = jnp.zeros_like(dc_scr)

        def step(diagonal):
            q, k, v, dout = q_ref[...], k_ref[...], v_ref[...], do_ref[...]
            lo = lax.broadcasted_iota(jnp.int32, (t, PAIR_W), 1) < HEAD_DIM
            zero = jnp.zeros_like(q)
            dq_new = jnp.zeros((t, PAIR_W), F32)
            for j in range(2):
                sel = lo if j == 0 else jnp.logical_not(lo)
                qj, kj, doj = jnp.where(sel, q, zero), jnp.where(sel, k, zero), jnp.where(sel, dout, zero)
                p = _head_probs(qj, k, cq_ref[:, j:j + 1], c_ref[j:j + 1, :], lse_ref[:, j:j + 1], diagonal)
                dp = _dot_nt(doj, v)
                ds = p * (dp - dl_ref[:, j:j + 1])
                dc_scr[j:j + 1, :] -= _colsum(ds)
                dsb = ds.astype(BF16)
                dv_scr[...] += _dot_tn(p.astype(BF16), doj)
                dk_scr[...] += _dot_tn(dsb, qj)
                dq_new = dq_new + _dot(dsb, kj)
            rows = pl.ds(pl.multiple_of(qi * t, t), t)
            dq_scr[rows, :] += dq_new

        @pl.when(qi > ki)
        def _():
            step(False)

        @pl.when(qi == ki)
        def _():
            step(True)
            rows = pl.ds(pl.multiple_of(qi * t, t), t)
            dq_ref[...] = (dq_scr[rows, :] * (HEAD_DIM ** -0.5)).astype(BF16)

        @pl.when(qi == nq - 1)
        def _():
            dk_ref[...] = dk_scr[...].astype(BF16)
            dv_ref[...] = dv_scr[...].astype(BF16)
            dc_ref[...] = dc_scr[0:2, :]

    qside = lambda w, off: pl.BlockSpec((t, w), lambda p, ki, qi: (jnp.maximum(qi, ki), off + p))
    kside = lambda off: pl.BlockSpec((t, PAIR_W), lambda p, ki, qi: (ki, off + p))
    return pl.pallas_call(
        body, name=name, grid=(N_PAIRS, nq, nq),
        in_specs=[qside(PAIR_W, 0), kside(N_PAIRS), kside(2 * N_PAIRS), qside(PAIR_W, 0),
                  pl.BlockSpec((None, 2, t), lambda p, ki, qi: (p, 0, ki)), qside(PAIR_W, 0), qside(PAIR_W, 0),
                  qside(PAIR_W, 0)],
        out_specs=[kside(0), kside(0), kside(0), pl.BlockSpec((None, 2, t), lambda p, ki, qi: (p, 0, ki))],
        out_shape=[SDS((lp, D_ATTN), BF16), SDS((lp, D_ATTN), BF16), SDS((lp, D_ATTN), BF16),
                   SDS((N_PAIRS, 2, lp), F32)],
        scratch_shapes=[pltpu.VMEM((lp, PAIR_W), F32), pltpu.VMEM((t, PAIR_W), F32), pltpu.VMEM((t, PAIR_W), F32),
                        pltpu.VMEM((8, t), F32)],
        compiler_params=_cparams(("parallel", "arbitrary", "arbitrary"), 48),
    )(qkv, qkv, qkv, do, crow, cpair, lse, delta)


def _shard_sum(own, parts, name, slot_order=False):
    rows, cols = own.shape
    tr = rows if rows <= 512 else 512
    assert rows % tr == 0

    def body(me_ref, own_ref, parts_ref, o_ref):
        if slot_order:
            acc = parts_ref[0].astype(F32)
            for j in range(1, 4):
                acc = acc + parts_ref[j].astype(F32)
        else:
            me = me_ref[0]
            acc = own_ref[...]
            for j in range(4):
                acc = acc + jnp.where(me == j, 0.0, parts_ref[j].astype(F32))
        o_ref[...] = acc

    me = (2 * lax.axis_index("x") + lax.axis_index("y")).astype(jnp.int32).reshape(1)
    return pl.pallas_call(
        body, name=name,
        grid_spec=pltpu.PrefetchScalarGridSpec(
            num_scalar_prefetch=1, grid=(rows // tr,),
            in_specs=[pl.BlockSpec((tr, cols), lambda i, me: (i, 0)), pl.BlockSpec((4, tr, cols), lambda i, me: (0, i, 0))],
            out_specs=pl.BlockSpec((tr, cols), lambda i, me: (i, 0))),
        out_shape=SDS((rows, cols), F32),
        compiler_params=_cparams(("parallel",), 48),
    )(me, own, parts)


def _adamw(w, m, v, s_own, s_sib, name):
    rows, cols = w.shape
    tr = rows if rows <= 256 else 256
    assert rows % tr == 0

    def body(w_ref, m_ref, v_ref, a_ref, b_ref, g_ref, d_ref, nm_ref, nv_ref):
        g = a_ref[...] + b_ref[...]
        g_ref[...] = g
        m = ADAM_B1 * m_ref[...] + (1.0 - ADAM_B1) * g
        v = ADAM_B2 * v_ref[...] + (1.0 - ADAM_B2) * jnp.square(g)
        nm_ref[...] = m
        nv_ref[...] = v
        m_hat = m / (1.0 - ADAM_B1 ** ADAM_STEP)
        v_hat = v / (1.0 - ADAM_B2 ** ADAM_STEP)
        d_ref[...] = -ADAM_LR * (m_hat / (jnp.sqrt(v_hat) + ADAM_EPS) + ADAM_WD * w_ref[...])

    blk = pl.BlockSpec((tr, cols), lambda i: (i, 0))
    return pl.pallas_call(
        body, name=name, grid=(rows // tr,), in_specs=[blk] * 5, out_specs=[blk] * 4,
        out_shape=[SDS((rows, cols), F32)] * 4,
        compiler_params=_cparams(("parallel",), 48),
    )(w, m, v, s_own, s_sib)


def _pad_rows(a, rows):
    return jnp.pad(a, ((0, rows - a.shape[0]), (0, 0)))


def _flat_rows(arrs, cols, rows):
    flat = jnp.concatenate([a.reshape(-1) for a in arrs])
    return jnp.pad(flat, (0, rows * cols - flat.shape[0])).reshape(rows, cols)


def _unflat(buf, shapes):
    flat, out, off = buf.reshape(-1), [], 0
    for s in shapes:
        n = 1
        for d in s:
            n *= d
        out.append(flat[off:off + n].reshape(s))
        off += n
    return out


def kernel(x, meta_tokens, mix_norm_g, w_in, b_forget, w_conf_dw, b_conf_dw, conf_ln_g, conf_ln_b, w_conf_pw, b_conf_pw, w_sc_conv, w_out, mlp_norm_g, w_mlp1, w_mlp2, final_norm_g, loss_target, m_meta_tokens, m_mix_norm_g, m_w_in, m_b_forget, m_w_conf_dw, m_b_conf_dw, m_conf_ln_g, m_conf_ln_b, m_w_conf_pw, m_b_conf_pw, m_w_sc_conv, m_w_out, m_mlp_norm_g, m_w_mlp1, m_w_mlp2, m_final_norm_g, v_meta_tokens, v_mix_norm_g, v_w_in, v_b_forget, v_w_conf_dw, v_b_conf_dw, v_conf_ln_g, v_conf_ln_b, v_w_conf_pw, v_b_conf_pw, v_w_sc_conv, v_w_out, v_mlp_norm_g, v_w_mlp1, v_w_mlp2, v_final_norm_g):
    depth = w_in.shape[0]
    seq = x.shape[1]
    l_tok = N_META + seq
    lp = _padded_len(l_tok)
    n_in = w_in.shape[2] * 4
    n_in_sh = w_in.shape[2]
    me = 2 * lax.axis_index("x") + lax.axis_index("y")

    small_shapes = [w_conf_dw.shape, w_conf_pw.shape, w_sc_conv.shape, meta_tokens.shape]
    n_small = sum(a.size for a in (w_conf_dw, w_conf_pw, w_sc_conv, meta_tokens))
    small_rows = -(-n_small // 1024 // 8) * 8
    small = _flat_rows([w_conf_dw, w_conf_pw, w_sc_conv, meta_tokens], 1024, small_rows)
    g_in, g_out, g_m1, g_m2, g_small = _chip_exchange(
        [w_in.astype(BF16).reshape(depth * D_MODEL, n_in_sh), w_out.astype(BF16).reshape(-1, D_MODEL),
         w_mlp1.astype(BF16).reshape(depth * D_MODEL, -1), w_mlp2.astype(BF16).reshape(-1, D_MODEL), small],
        gather=True, name="weights_all_gather")
    w_in_f = jnp.concatenate([g_in[j].reshape(depth, D_MODEL, n_in_sh) for j in range(4)], axis=2)
    w_out_f = jnp.concatenate([g_out[j].reshape(depth, -1, D_MODEL) for j in range(4)], axis=1)
    w_m1_f = jnp.concatenate([g_m1[j].reshape(depth, D_MODEL, -1) for j in range(4)], axis=2)
    w_m2_f = jnp.concatenate([g_m2[j].reshape(depth, -1, D_MODEL) for j in range(4)], axis=1)
    sm = [_unflat(g_small[j], small_shapes) for j in range(4)]
    w_dw_f = jnp.concatenate([s[0] for s in sm], axis=2)
    w_pw_f = jnp.concatenate([s[1] for s in sm], axis=1)
    w_sc_f = jnp.concatenate([s[2] for s in sm], axis=2)
    meta_f = jnp.concatenate([s[3] for s in sm], axis=1)

    q_end, f_end = 3 * D_ATTN, 3 * D_ATTN + N_HEADS
    wqkv = [w_in_f[l, :, :q_end] for l in range(depth)]
    wrest = [jnp.concatenate([w_in_f[l, :, f_end:], w_in_f[l, :, q_end:f_end],
                              jnp.zeros((D_MODEL, N_F_PAD - N_HEADS), BF16)], axis=1) for l in range(depth)]
    wdw = [_pad_rows(w_dw_f[l], 32) for l in range(depth)]
    wsc = [_pad_rows(w_sc_f[l], 8) for l in range(depth)]
    wpw = [w_pw_f[l].astype(BF16) for l in range(depth)]
    vec = [_pad_rows(jnp.stack([b_conf_dw[l], conf_ln_g[l], conf_ln_b[l], b_conf_pw[l],
                                jnp.pad(b_forget[l], (0, D_CONF - N_HEADS))]), 8) for l in range(depth)]

    h = jnp.concatenate([meta_f, x[0], jnp.zeros((lp - l_tok, D_MODEL), F32)], axis=0)
    target = jnp.pad(loss_target[0], ((N_META, lp - l_tok), (0, 0)))
    saved = []
    for l in range(depth):
        g1, g2 = mix_norm_g[l][None, :], mlp_norm_g[l][None, :]
        hn1, qkv, rest = _proj_fwd(h, g1, wqkv[l], wrest[l], name=f"proj_fwd_{l}")
        cs, dwc, cv, ccol = _conv_fwd(rest, wdw[l], vec[l], wpw[l], wsc[l], name=f"conv_fwd_{l}")
        crow = ccol[:, :N_HEADS].T.reshape(N_PAIRS, 2, lp)
        cpair = jnp.pad(ccol[:, :N_HEADS].reshape(lp, N_PAIRS, 2), ((0, 0), (0, 0), (0, PAIR_W - 2))).reshape(lp, D_ATTN)
        attn, lse = _attn_fwd(qkv, crow, cpair, name=f"attn_fwd_{l}")
        h1 = _mix_out(h, attn, cs, w_out_f[l, :D_ATTN], w_out_f[l, D_ATTN:], name=f"mix_out_{l}")
        hn2, a, r, h2 = _mlp_fwd(h1, g2, w_m1_f[l], w_m2_f[l], name=f"mlp_fwd_{l}")
        saved.append((h, g1, g2, hn1, qkv, rest, cs, dwc, cv, crow, cpair, attn, lse, h1, hn2, a, r))
        h = h2
    dh, loss11, d_final_g = _loss_head(h, target, final_norm_g[None, :], seq, name="loss_head")
    loss = lax.psum(loss11[0, 0], ("x", "y", "c"))

    grads = [None] * depth
    for l in reversed(range(depth)):
        h0, g1, g2, hn1, qkv, rest, cs, dwc, cv, crow, cpair, attn, lse, h1, hn2, a, r = saved[l]
        da, dh1, dg2 = _mlp_bwd(dh, h1, g2, a, w_m1_f[l], w_m2_f[l], name=f"mlp_bwd_{l}")
        d_w2 = _mm_tn(r, dh, name=f"dw_mlp2_{l}")
        d_w1 = _mm_tn(hn2, da, name=f"dw_mlp1_{l}")
        do, dcs = _mix_bwd(dh1, w_out_f[l, :D_ATTN], w_out_f[l, D_ATTN:], name=f"mix_bwd_{l}")
        delta = _attn_delta(qkv, do, crow, cpair, lse, name=f"attn_delta_{l}")
        d_wout = jnp.concatenate([_mm_tn(attn, dh1, name=f"dw_out_a_{l}"), _mm_tn(cs, dh1, name=f"dw_out_c_{l}")], axis=0)
        dq, dk, dv, dcrow = _attn_bwd(qkv, do, crow, cpair, lse, delta, name=f"attn_bwd_{l}")
        dccol = jnp.pad(dcrow.reshape(N_HEADS, lp).T, ((0, 0), (0, N_F_PAD - N_HEADS)))
        drest, d_wdw, d_wsc, d_wpw, d_vec = _conv_bwd(dcs, dwc, cv, rest, dccol, wdw[l], vec[l], wpw[l], wsc[l],
                                                      name=f"conv_bwd_{l}")
        dh, dg1 = _in_bwd(dh1, h0, g1, dq, dk, dv, drest, wqkv[l], wrest[l], name=f"in_bwd_{l}")
        d_wq = _mm_tn(hn1, dq, name=f"dw_q_{l}")
        d_wk = _mm_tn(hn1, dk, name=f"dw_k_{l}")
        d_wv = _mm_tn(hn1, dv, name=f"dw_v_{l}")
        d_wr = _mm_tn(hn1, drest, name=f"dw_rest_{l}")
        d_win = jnp.concatenate([d_wq, d_wk, d_wv, d_wr[:, 5 * D_CONF:5 * D_CONF + N_HEADS], d_wr[:, :5 * D_CONF]], axis=1)
        grads[l] = dict(w_in=d_win, w_out=d_wout, w_mlp1=d_w1, w_mlp2=d_w2, mix_norm_g=dg1[0], mlp_norm_g=dg2[0],
                        w_conf_dw=d_wdw[:CONF_K], w_sc_conv=d_wsc[:SC_K], w_conf_pw=d_wpw, b_conf_dw=d_vec[0],
                        conf_ln_g=d_vec[1], conf_ln_b=d_vec[2], b_conf_pw=d_vec[3], b_forget=d_vec[4, :N_HEADS])
    grad_x = dh[N_META:N_META + seq][None]
    d_meta = dh[:N_META]

    def stack(k):
        return jnp.stack([grads[l][k] for l in range(depth)])

    def col_shards(a):
        d, rr, cc = a.shape
        return a.reshape(d, rr, 4, cc // 4).transpose(2, 0, 1, 3).reshape(4, d * rr, cc // 4)

    def row_shards(a):
        d, rr, cc = a.shape
        return a.reshape(d, 4, rr // 4, cc).transpose(1, 0, 2, 3).reshape(4, d * rr // 4, cc)

    big = [col_shards(stack("w_in")), row_shards(stack("w_out")), col_shards(stack("w_mlp1")), row_shards(stack("w_mlp2"))]
    rep = [stack("mix_norm_g"), stack("b_forget"), stack("b_conf_dw"), stack("conf_ln_g"), stack("conf_ln_b"),
           stack("b_conf_pw"), stack("mlp_norm_g"), d_final_g[0]]
    rep_shapes = [a.shape for a in rep]
    sh_dw, sh_pw, sh_sc = col_shards(stack("w_conf_dw")), row_shards(stack("w_conf_pw")), col_shards(stack("w_sc_conv"))
    sh_meta = d_meta.reshape(N_META, 4, D_MODEL // 4).transpose(1, 0, 2)
    n_rep = sum(a.size for a in rep)
    gs_rows = -(-(n_small + n_rep) // 1024 // 8) * 8
    gsmall = jnp.stack([_flat_rows([sh_dw[j], sh_pw[j], sh_sc[j], sh_meta[j]] + rep, 1024, gs_rows) for j in range(4)])
    recv = _chip_exchange([b.astype(BF16) for b in big] + [gsmall], gather=False, name="grads_exchange")
    own = [lax.dynamic_index_in_dim(b, me, 0, keepdims=False) for b in big + [gsmall]]
    sums = [_shard_sum(own[i], recv[i], name=f"shard_sum_{i}", slot_order=(i == 4)) for i in range(5)]
    sibs = _sibling_exchange(sums, name="sibling_exchange")

    w_small = _flat_rows([w_conf_dw, w_conf_pw, w_sc_conv, meta_tokens, mix_norm_g, b_forget, b_conf_dw, conf_ln_g,
                          conf_ln_b, b_conf_pw, mlp_norm_g, final_norm_g], 1024, gs_rows)
    m_small = _flat_rows([m_w_conf_dw, m_w_conf_pw, m_w_sc_conv, m_meta_tokens, m_mix_norm_g, m_b_forget, m_b_conf_dw,
                          m_conf_ln_g, m_conf_ln_b, m_b_conf_pw, m_mlp_norm_g, m_final_norm_g], 1024, gs_rows)
    v_small = _flat_rows([v_w_conf_dw, v_w_conf_pw, v_w_sc_conv, v_meta_tokens, v_mix_norm_g, v_b_forget, v_b_conf_dw,
                          v_conf_ln_g, v_conf_ln_b, v_b_conf_pw, v_mlp_norm_g, v_final_norm_g], 1024, gs_rows)
    triples = [(w_in, m_w_in, v_w_in), (w_out, m_w_out, v_w_out), (w_mlp1, m_w_mlp1, v_w_mlp1),
               (w_mlp2, m_w_mlp2, v_w_mlp2), (w_small, m_small, v_small)]
    res = []
    for i, (w, m, v) in enumerate(triples):
        shp = w.shape
        two_d = sums[i].shape
        out4 = _adamw(w.reshape(two_d), m.reshape(two_d), v.reshape(two_d), sums[i], sibs[i], name=f"adamw_{i}")
        res.append([o.reshape(shp) for o in out4])
    small_out = [_unflat(res[4][k], small_shapes + rep_shapes) for k in range(4)]

    names = ["meta_tokens", "mix_norm_g", "w_in", "b_forget", "w_conf_dw", "b_conf_dw", "conf_ln_g", "conf_ln_b",
             "w_conf_pw", "b_conf_pw", "w_sc_conv", "w_out", "mlp_norm_g", "w_mlp1", "w_mlp2", "final_norm_g"]
    small_order = ["w_conf_dw", "w_conf_pw", "w_sc_conv", "meta_tokens", "mix_norm_g", "b_forget", "b_conf_dw",
                   "conf_ln_g", "conf_ln_b", "b_conf_pw", "mlp_norm_g", "final_norm_g"]
    big_order = {"w_in": 0, "w_out": 1, "w_mlp1": 2, "w_mlp2": 3}
    outs = []
    for k in range(4):
        for nme in names:
            outs.append(res[big_order[nme]][k] if nme in big_order else small_out[k][small_order.index(nme)])
    return (loss, grad_x, *outs)
```

```python
import jax
import jax.numpy as jnp
from jax import lax
from jax.experimental import pallas as pl
from jax.experimental.pallas import tpu as pltpu

F32, BF16 = jnp.float32, jnp.bfloat16
D_MODEL = 1024
D_ATTN = 512
D_CONF = 256
N_HEADS = 8
HEAD_DIM = 64
N_PAIRS = N_HEADS // 2
PAIR_W = 2 * HEAD_DIM
N_META = 16
D_FF = 4096
CONF_K = 31
SC_K = 3
HALO = 32
N_F_PAD = 128
N_REST = 5 * D_CONF + N_F_PAD
F_BLK = 5 * D_CONF // N_F_PAD
EPS = 1e-6
NEG = -1e30
BIG_TILE = 640
SMALL_TILE = 128
MLP_TILE_DIV = 2
V7X_VMEM_BYTES = 64 * 1024 * 1024
ADAM_LR, ADAM_B1, ADAM_B2, ADAM_EPS, ADAM_WD, ADAM_STEP = 0.001, 0.9, 0.999, 1e-08, 0.01, 10
MESH = pl.DeviceIdType.MESH
HIGHEST = lax.Precision.HIGHEST
SDS = jax.ShapeDtypeStruct


def _cparams(sem=None, vmem_mb=None):
    kw = {}
    if sem is not None:
        kw["dimension_semantics"] = sem
    if vmem_mb is not None:
        assert vmem_mb * 1024 * 1024 <= V7X_VMEM_BYTES
        kw["vmem_limit_bytes"] = vmem_mb * 1024 * 1024
    return pltpu.CompilerParams(**kw)


def _dot(a, b):
    return jnp.dot(a, b, preferred_element_type=F32)


def _dot_nt(a, b):
    return lax.dot_general(a, b, (((1,), (1,)), ((), ())), preferred_element_type=F32)


def _dot_tn(a, b):
    return lax.dot_general(a, b, (((0,), (0,)), ((), ())), preferred_element_type=F32)


def _sigmoid(x):
    return 1.0 / (1.0 + jnp.exp(-x))


def _rms_fwd(x, g):
    r = lax.rsqrt(jnp.mean(x * x, axis=-1, keepdims=True) + EPS)
    xn = x * r
    return r, xn, xn * g


def _rms_bwd(r, xn, g, dy):
    u = dy * g
    return r * (u - xn * jnp.mean(u * xn, axis=-1, keepdims=True))


def _colsum(x):
    return jnp.sum(x, axis=0, keepdims=True)


def _row_tile(lp):
    return BIG_TILE if lp % BIG_TILE == 0 else SMALL_TILE


def _padded_len(l):
    t = BIG_TILE if l >= BIG_TILE else SMALL_TILE
    return -(-l // t) * t


def _chip_exchange(arrs, gather, name):
    n = len(arrs)

    def body(*refs):
        ins, outs = refs[:n], refs[n:2 * n]
        send_sems, recv_sems, local_sems = refs[2 * n:]
        x, y, c = lax.axis_index("x"), lax.axis_index("y"), lax.axis_index("c")
        me = 2 * x + y
        chips = [(1 - x, y), (x, 1 - y), (1 - x, 1 - y)]
        local, sends, recvs = [], [], []
        for i in range(n):
            src_me = ins[i] if gather else ins[i].at[me]
            local.append(pltpu.make_async_copy(src_me, outs[i].at[me], local_sems.at[i]))
            for j, (px, py) in enumerate(chips):
                k = 3 * i + j
                peer = 2 * px + py
                src = ins[i] if gather else ins[i].at[peer]
                sends.append(pltpu.make_async_remote_copy(
                    src_ref=src, dst_ref=outs[i].at[me], send_sem=send_sems.at[k], recv_sem=recv_sems.at[k],
                    device_id=(px, py, c), device_id_type=MESH))
                recvs.append(pltpu.make_async_remote_copy(
                    src_ref=src, dst_ref=outs[i].at[peer], send_sem=send_sems.at[k], recv_sem=recv_sems.at[k],
                    device_id=(px, py, c), device_id_type=MESH))
        for cp in local + sends:
            cp.start()
        for cp in recvs:
            cp.wait_recv()
        for cp in sends:
            cp.wait_send()
        for cp in local:
            cp.wait()

    any_spec = pl.BlockSpec(memory_space=pl.ANY)
    out_shape = [SDS((4,) + a.shape, a.dtype) if gather else SDS(a.shape, a.dtype) for a in arrs]
    return pl.pallas_call(
        body, name=name, out_shape=out_shape,
        in_specs=[any_spec] * n, out_specs=[any_spec] * n,
        scratch_shapes=[pltpu.SemaphoreType.DMA((3 * n,)), pltpu.SemaphoreType.DMA((3 * n,)),
                        pltpu.SemaphoreType.DMA((n,))],
    )(*arrs)


def _sibling_exchange(arrs, name):
    n = len(arrs)

    def body(*refs):
        ins, outs = refs[:n], refs[n:2 * n]
        send_sems, recv_sems = refs[2 * n:]
        sib = (lax.axis_index("x"), lax.axis_index("y"), 1 - lax.axis_index("c"))
        cps = [pltpu.make_async_remote_copy(src_ref=ins[i], dst_ref=outs[i], send_sem=send_sems.at[i],
                                            recv_sem=recv_sems.at[i], device_id=sib, device_id_type=MESH)
               for i in range(n)]
        for cp in cps:
            cp.start()
        for cp in cps:
            cp.wait_recv()
        for cp in cps:
            cp.wait_send()

    any_spec = pl.BlockSpec(memory_space=pl.ANY)
    return pl.pallas_call(
        body, name=name, out_shape=[SDS(a.shape, a.dtype) for a in arrs],
        in_specs=[any_spec] * n, out_specs=[any_spec] * n,
        scratch_shapes=[pltpu.SemaphoreType.DMA((n,)), pltpu.SemaphoreType.DMA((n,))],
    )(*arrs)


def _proj_fwd(h, g, wqkv, wrest, name):
    lp = h.shape[0]
    tm = _row_tile(lp)

    def body(h_ref, g_ref, wq_ref, wr_ref, hn_ref, qkv_ref, rest_ref):
        _, _, y = _rms_fwd(h_ref[...], g_ref[...])
        hn = y.astype(BF16)
        hn_ref[...] = hn
        qkv = _dot(hn, wq_ref[...])
        qkv_ref[:, :D_ATTN] = (qkv[:, :D_ATTN] * (HEAD_DIM ** -0.5)).astype(BF16)
        qkv_ref[:, D_ATTN:] = qkv[:, D_ATTN:].astype(BF16)
        rest_ref[...] = _dot(hn, wr_ref[...])

    row = lambda w: pl.BlockSpec((tm, w), lambda i: (i, 0))
    full = lambda a: pl.BlockSpec(a.shape, lambda i: (0, 0))
    return pl.pallas_call(
        body, name=name, grid=(lp // tm,),
        in_specs=[row(D_MODEL), full(g), full(wqkv), full(wrest)],
        out_specs=[row(D_MODEL), row(3 * D_ATTN), row(N_REST)],
        out_shape=[SDS((lp, D_MODEL), BF16), SDS((lp, 3 * D_ATTN), BF16), SDS((lp, N_REST), F32)],
        compiler_params=_cparams(("parallel",), 48),
    )(h, g, wqkv, wrest)


def _mm_tn(a, b, name):
    kk, m = a.shape
    n = b.shape[1]
    tk = _row_tile(kk)
    tmo = min(m, 1024)
    tno = n if n <= 1536 else 1024

    def body(a_ref, b_ref, o_ref):
        @pl.when(pl.program_id(2) == 0)
        def _():
            o_ref[...] = jnp.zeros_like(o_ref)
        o_ref[...] += _dot_tn(a_ref[...], b_ref[...].astype(BF16))

    return pl.pallas_call(
        body, name=name, grid=(m // tmo, n // tno, kk // tk),
        in_specs=[pl.BlockSpec((tk, tmo), lambda i, j, k: (k, i)), pl.BlockSpec((tk, tno), lambda i, j, k: (k, j))],
        out_specs=pl.BlockSpec((tmo, tno), lambda i, j, k: (i, j)),
        out_shape=SDS((m, n), F32),
        compiler_params=_cparams(("parallel", "parallel", "arbitrary"), 48),
    )(a, b)


def _mix_out(h, attn, cs, wa, wc, name):
    lp = h.shape[0]
    tm = _row_tile(lp)

    def body(h_ref, at_ref, cs_ref, wa_ref, wc_ref, o_ref):
        o_ref[...] = h_ref[...] + _dot(at_ref[...], wa_ref[...]) + _dot(cs_ref[...], wc_ref[...])

    row = lambda w: pl.BlockSpec((tm, w), lambda i: (i, 0))
    full = lambda a: pl.BlockSpec(a.shape, lambda i: (0, 0))
    return pl.pallas_call(
        body, name=name, grid=(lp // tm,),
        in_specs=[row(D_MODEL), row(D_ATTN), row(D_ATTN), full(wa), full(wc)],
        out_specs=row(D_MODEL), out_shape=SDS((lp, D_MODEL), F32),
        compiler_params=_cparams(("parallel",), 48),
    )(h, attn, cs, wa, wc)


def _mlp_fwd(h, g, w1, w2, name):
    lp = h.shape[0]
    tm = _row_tile(lp) // MLP_TILE_DIV

    def body(h_ref, g_ref, w1_ref, w2_ref, hn_ref, a_ref, r_ref, o_ref):
        x = h_ref[...]
        _, _, y = _rms_fwd(x, g_ref[...])
        hn = y.astype(BF16)
        hn_ref[...] = hn
        a = _dot(hn, w1_ref[...])
        a_ref[...] = a
        r = jnp.square(jnp.maximum(a, 0.0)).astype(BF16)
        r_ref[...] = r
        o_ref[...] = x + _dot(r, w2_ref[...])

    row = lambda w: pl.BlockSpec((tm, w), lambda i: (i, 0))
    vmem = pl.BlockSpec(memory_space=pltpu.VMEM)
    return pl.pallas_call(
        body, name=name, grid=(lp // tm,),
        in_specs=[row(D_MODEL), pl.BlockSpec(g.shape, lambda i: (0, 0)), vmem, vmem],
        out_specs=[row(D_MODEL), row(D_FF), row(D_FF), row(D_MODEL)],
        out_shape=[SDS((lp, D_MODEL), BF16), SDS((lp, D_FF), F32), SDS((lp, D_FF), BF16), SDS((lp, D_MODEL), F32)],
        compiler_params=_cparams(("parallel",), 60),
    )(h, g, w1, w2)


def _mlp_bwd(dh2, h1, g, a, w1, w2, name):
    lp = h1.shape[0]
    tm = _row_tile(lp) // MLP_TILE_DIV

    def body(dy_ref, h_ref, g_ref, a_ref, w1_ref, w2_ref, da_ref, dh_ref, dg_ref):
        @pl.when(pl.program_id(0) == 0)
        def _():
            dg_ref[...] = jnp.zeros_like(dg_ref)
        dy = dy_ref[...]
        dr = _dot_nt(dy.astype(BF16), w2_ref[...])
        da = (dr * (2.0 * jnp.maximum(a_ref[...], 0.0))).astype(BF16)
        da_ref[...] = da
        dhn = _dot_nt(da, w1_ref[...])
        r, xn, _ = _rms_fwd(h_ref[...], g_ref[...])
        dg_ref[...] += _colsum(dhn * xn)
        dh_ref[...] = dy + _rms_bwd(r, xn, g_ref[...], dhn)

    row = lambda w: pl.BlockSpec((tm, w), lambda i: (i, 0))
    vmem = pl.BlockSpec(memory_space=pltpu.VMEM)
    vec = pl.BlockSpec((1, D_MODEL), lambda i: (0, 0))
    return pl.pallas_call(
        body, name=name, grid=(lp // tm,),
        in_specs=[row(D_MODEL), row(D_MODEL), vec, row(D_FF), vmem, vmem],
        out_specs=[row(D_FF), row(D_MODEL), vec],
        out_shape=[SDS((lp, D_FF), BF16), SDS((lp, D_MODEL), F32), SDS((1, D_MODEL), F32)],
        compiler_params=_cparams(("arbitrary",), 60),
    )(dh2, h1, g, a, w1, w2)


def _mix_bwd(dh1, wa, wc, attn, seg, name):
    lp = dh1.shape[0]
    tm = _row_tile(lp)

    def body(dy_ref, wa_ref, wc_ref, o_ref, seg_ref, do_ref, dcs_ref, dl_ref):
        dy = dy_ref[...].astype(BF16)
        do = _dot_nt(dy, wa_ref[...])
        do_ref[...] = do.astype(BF16)
        dcs_ref[...] = _dot_nt(dy, wc_ref[...])
        dl_ref[...] = jnp.dot(do * o_ref[...].astype(F32), seg_ref[...], precision=HIGHEST,
                              preferred_element_type=F32)

    row = lambda w: pl.BlockSpec((tm, w), lambda i: (i, 0))
    full = lambda x: pl.BlockSpec(x.shape, lambda i: (0, 0))
    return pl.pallas_call(
        body, name=name, grid=(lp // tm,),
        in_specs=[row(D_MODEL), full(wa), full(wc), row(D_ATTN), full(seg)],
        out_specs=[row(D_ATTN), row(D_ATTN), row(D_ATTN)],
        out_shape=[SDS((lp, D_ATTN), BF16), SDS((lp, D_ATTN), F32), SDS((lp, D_ATTN), F32)],
        compiler_params=_cparams(("parallel",), 48),
    )(dh1, wa, wc, attn, seg)


def _in_bwd(dh1, h, g, dq, dk, dv, drest, wqkv, wrest, name):
    lp = h.shape[0]
    tm = _row_tile(lp)

    def body(dy_ref, h_ref, g_ref, dq_ref, dk_ref, dv_ref, dr_ref, wq_ref, wr_ref, dh_ref, dg_ref):
        @pl.when(pl.program_id(0) == 0)
        def _():
            dg_ref[...] = jnp.zeros_like(dg_ref)
        dhn = (_dot_nt(dq_ref[...], wq_ref[:, 0:D_ATTN]) + _dot_nt(dk_ref[...], wq_ref[:, D_ATTN:2 * D_ATTN])
               + _dot_nt(dv_ref[...], wq_ref[:, 2 * D_ATTN:]) + _dot_nt(dr_ref[...], wr_ref[...]))
        r, xn, _ = _rms_fwd(h_ref[...], g_ref[...])
        dg_ref[...] += _colsum(dhn * xn)
        dh_ref[...] = dy_ref[...] + _rms_bwd(r, xn, g_ref[...], dhn)

    row = lambda w: pl.BlockSpec((tm, w), lambda i: (i, 0))
    full = lambda x: pl.BlockSpec(x.shape, lambda i: (0, 0))
    vec = pl.BlockSpec((1, D_MODEL), lambda i: (0, 0))
    return pl.pallas_call(
        body, name=name, grid=(lp // tm,),
        in_specs=[row(D_MODEL), row(D_MODEL), vec, row(D_ATTN), row(D_ATTN), row(D_ATTN), row(N_REST),
                  full(wqkv), full(wrest)],
        out_specs=[row(D_MODEL), vec],
        out_shape=[SDS((lp, D_MODEL), F32), SDS((1, D_MODEL), F32)],
        compiler_params=_cparams(("arbitrary",), 48),
    )(dh1, h, g, dq, dk, dv, drest, wqkv, wrest)


def _loss_head(h, target, g, seq, name):
    lp = h.shape[0]
    tm = _row_tile(lp)

    def body(h_ref, t_ref, g_ref, dh_ref, loss_ref, dg_ref):
        i = pl.program_id(0)

        @pl.when(i == 0)
        def _():
            dg_ref[...] = jnp.zeros_like(dg_ref)
            loss_ref[...] = jnp.zeros_like(loss_ref)
        r, xn, y = _rms_fwd(h_ref[...], g_ref[...])
        rows = i * tm + lax.broadcasted_iota(jnp.int32, (tm, 1), 0)
        real = (rows >= N_META) & (rows < N_META + seq)
        e = jnp.where(real, y - t_ref[...], 0.0)
        loss_ref[...] += 0.5 * jnp.sum(jnp.mean(e * e, axis=-1, keepdims=True), axis=0, keepdims=True)
        dy = e * (1.0 / D_MODEL)
        dg_ref[...] += _colsum(dy * xn)
        dh_ref[...] = _rms_bwd(r, xn, g_ref[...], dy)

    row = pl.BlockSpec((tm, D_MODEL), lambda i: (i, 0))
    vec = pl.BlockSpec((1, D_MODEL), lambda i: (0, 0))
    return pl.pallas_call(
        body, name=name, grid=(lp // tm,),
        in_specs=[row, row, vec],
        out_specs=[row, pl.BlockSpec((1, 1), lambda i: (0, 0)), vec],
        out_shape=[SDS((lp, D_MODEL), F32), SDS((1, 1), F32), SDS((1, D_MODEL), F32)],
        compiler_params=_cparams(("arbitrary",), 48),
    )(h, target, g)


def _shift_down(xcat, j):
    return (xcat if j == 0 else pltpu.roll(xcat, j, 0))[HALO:, :]


def _shift_up(xcat, j, tm):
    n = xcat.shape[0]
    return (xcat if j == 0 else pltpu.roll(xcat, n - j, 0))[:tm, :]


def _layernorm_parts(x):
    mu = jnp.mean(x, axis=-1, keepdims=True)
    xc = x - mu
    rstd = lax.rsqrt(jnp.mean(xc * xc, axis=-1, keepdims=True) + EPS)
    return rstd, xc * rstd


def _log_sigmoid(z):
    return jnp.minimum(z, 0.0) - jnp.log(1.0 + jnp.exp(-jnp.abs(z)))


def _conv_fwd(rest, wdw, vec, wpw, wsc, name):
    lp = rest.shape[0]
    tm = _row_tile(lp)
    per = tm // HALO

    def body(a_ref, g_ref, ah_ref, gh_ref, b_ref, c_ref, u_ref, ch_ref, uh_ref, f_ref,
             wdw_ref, vec_ref, wpw_ref, wsc_ref, cs_ref, dwc_ref, cv_ref, cc_ref, carry):
        i = pl.program_id(0)
        first = i == 0

        @pl.when(first)
        def _():
            carry[...] = jnp.zeros_like(carry)
        glu = a_ref[...] * _sigmoid(g_ref[...])
        glu_h = jnp.where(first, 0.0, ah_ref[...] * _sigmoid(gh_ref[...]))
        xcat = jnp.concatenate([glu_h, glu], axis=0)
        acc = jnp.zeros((tm, D_CONF), F32) + vec_ref[0:1, :]
        for k in range(CONF_K):
            acc = acc + wdw_ref[k:k + 1, :] * _shift_down(xcat, CONF_K - 1 - k)
        dwc_ref[...] = acc
        _, y = _layernorm_parts(acc)
        ln = y * vec_ref[1:2, :] + vec_ref[2:3, :]
        sw = ln * _sigmoid(ln)
        conf = _dot(sw.astype(BF16), wpw_ref[...]) + vec_ref[3:4, :]
        p = c_ref[...] * u_ref[...]
        p_h = jnp.where(first, 0.0, ch_ref[...] * uh_ref[...])
        pcat = jnp.concatenate([p_h, p], axis=0)
        cv = jnp.zeros((tm, D_CONF), F32)
        for k in range(SC_K):
            cv = cv + wsc_ref[k:k + 1, :] * _shift_down(pcat, SC_K - 1 - k)
        cv_ref[...] = cv
        cs_ref[:, :D_CONF] = conf.astype(BF16)
        cs_ref[:, D_CONF:] = (b_ref[...] * cv).astype(BF16)
        logf = _log_sigmoid(f_ref[...] + vec_ref[4:5, :N_F_PAD])
        tri = (lax.broadcasted_iota(jnp.int32, (tm, tm), 0) >= lax.broadcasted_iota(jnp.int32, (tm, tm), 1))
        c = jnp.dot(tri.astype(F32), logf, precision=HIGHEST, preferred_element_type=F32) + carry[0:1, :]
        cc_ref[...] = c
        carry[0:1, :] = c[tm - 1:tm, :]

    cur = lambda j: pl.BlockSpec((tm, D_CONF), lambda i: (i, j))
    prev = lambda j: pl.BlockSpec((HALO, D_CONF), lambda i: (jnp.maximum(i * per - 1, 0), j))
    full = lambda x: pl.BlockSpec(x.shape, lambda i: (0, 0))
    return pl.pallas_call(
        body, name=name, grid=(lp // tm,),
        in_specs=[cur(0), cur(1), prev(0), prev(1), cur(2), cur(3), cur(4), prev(3), prev(4),
                  pl.BlockSpec((tm, N_F_PAD), lambda i: (i, F_BLK)),
                  full(wdw), full(vec), full(wpw), full(wsc)],
        out_specs=[pl.BlockSpec((tm, 2 * D_CONF), lambda i: (i, 0)), cur(0), cur(0),
                   pl.BlockSpec((tm, N_F_PAD), lambda i: (i, 0))],
        out_shape=[SDS((lp, 2 * D_CONF), BF16), SDS((lp, D_CONF), F32), SDS((lp, D_CONF), F32),
                   SDS((lp, N_F_PAD), F32)],
        scratch_shapes=[pltpu.VMEM((8, N_F_PAD), F32)],
        compiler_params=_cparams(("arbitrary",), 48),
    )(rest, rest, rest, rest, rest, rest, rest, rest, rest, rest, wdw, vec, wpw, wsc)


def _conv_bwd(dcs, dwc, cv, rest, dccol, wdw, vec, wpw, wsc, name):
    lp = rest.shape[0]
    tm = _row_tile(lp)
    per = tm // HALO
    nt = lp // tm
    n_halo_blocks = lp // HALO

    def body(dcf_ref, dsc_ref, dcfn_ref, dscn_ref, dwc_ref, dwcn_ref, cv_ref,
             a_ref, g_ref, ah_ref, gh_ref, b_ref, bn_ref, c_ref, u_ref, ch_ref, uh_ref, f_ref, dcc_ref,
             wdw_ref, vec_ref, wpw_ref, wsc_ref,
             dr_ref, dwdw_ref, dwsc_ref, dwpw_ref, dvec_ref, carry):
        i = pl.program_id(0)
        first = i == nt - 1
        last = i == 0

        @pl.when(i == 0)
        def _():
            carry[...] = jnp.zeros_like(carry)
            dwdw_ref[...] = jnp.zeros_like(dwdw_ref)
            dwsc_ref[...] = jnp.zeros_like(dwsc_ref)
            dwpw_ref[...] = jnp.zeros_like(dwpw_ref)
            dvec_ref[...] = jnp.zeros_like(dvec_ref)

        ln_g, ln_b = vec_ref[1:2, :], vec_ref[2:3, :]

        def ln_bwd(dconf, x):
            rstd, y = _layernorm_parts(x)
            ln = y * ln_g + ln_b
            sg = _sigmoid(ln)
            dsw = _dot_nt(dconf.astype(BF16), wpw_ref[...])
            dln = dsw * (sg * (1.0 + ln * (1.0 - sg)))
            dy = dln * ln_g
            ddw = rstd * (dy - jnp.mean(dy, axis=-1, keepdims=True) - y * jnp.mean(dy * y, axis=-1, keepdims=True))
            return ln * sg, y, dln, ddw

        dconf = dcf_ref[...]
        sw, y, dln, ddw = ln_bwd(dconf, dwc_ref[...])
        _, _, _, ddw_n = ln_bwd(dcfn_ref[...], dwcn_ref[...])
        ddw_n = jnp.where(last, 0.0, ddw_n)
        dvec_ref[0:1, :] += _colsum(ddw)
        dvec_ref[1:2, :] += _colsum(dln * y)
        dvec_ref[2:3, :] += _colsum(dln)
        dvec_ref[3:4, :] += _colsum(dconf)
        dwpw_ref[...] += _dot_tn(sw.astype(BF16), dconf.astype(BF16))

        a, sg = a_ref[...], _sigmoid(g_ref[...])
        glu = a * sg
        glu_h = jnp.where(first, 0.0, ah_ref[...] * _sigmoid(gh_ref[...]))
        xcat = jnp.concatenate([glu_h, glu], axis=0)
        dcat = jnp.concatenate([ddw, ddw_n], axis=0)
        dglu = jnp.zeros((tm, D_CONF), F32)
        for k in range(CONF_K):
            j = CONF_K - 1 - k
            dwdw_ref[k:k + 1, :] += _colsum(ddw * _shift_down(xcat, j))
            dglu = dglu + wdw_ref[k:k + 1, :] * _shift_up(dcat, j, tm)
        dr_ref[:, 0:D_CONF] = (dglu * sg).astype(BF16)
        dr_ref[:, D_CONF:2 * D_CONF] = (dglu * a * sg * (1.0 - sg)).astype(BF16)

        dsc = dsc_ref[...]
        b, c, u = b_ref[...], c_ref[...], u_ref[...]
        dcv = dsc * b
        dcv_n = jnp.where(last, 0.0, dscn_ref[...] * bn_ref[...])
        p_h = jnp.where(first, 0.0, ch_ref[...] * uh_ref[...])
        pcat = jnp.concatenate([p_h, c * u], axis=0)
        dccat = jnp.concatenate([dcv, dcv_n], axis=0)
        dp = jnp.zeros((tm, D_CONF), F32)
        for k in range(SC_K):
            j = SC_K - 1 - k
            dwsc_ref[k:k + 1, :] += _colsum(dcv * _shift_down(pcat, j))
            dp = dp + wsc_ref[k:k + 1, :] * _shift_up(dccat, j, tm)
        dr_ref[:, 2 * D_CONF:3 * D_CONF] = (dsc * cv_ref[...]).astype(BF16)
        dr_ref[:, 3 * D_CONF:4 * D_CONF] = (dp * u).astype(BF16)
        dr_ref[:, 4 * D_CONF:5 * D_CONF] = (dp * c).astype(BF16)

        tri = (lax.broadcasted_iota(jnp.int32, (tm, tm), 0) <= lax.broadcasted_iota(jnp.int32, (tm, tm), 1))
        dlogf = jnp.dot(tri.astype(F32), dcc_ref[...], precision=HIGHEST, preferred_element_type=F32) + carry[0:1, :]
        carry[0:1, :] = dlogf[0:1, :]
        dz = dlogf * _sigmoid(-(f_ref[...] + vec_ref[4:5, :N_F_PAD]))
        dvec_ref[4:5, 0:N_F_PAD] += _colsum(dz)
        dr_ref[:, 5 * D_CONF:] = dz.astype(BF16)

    rt = lambda i: nt - 1 - i
    cur = lambda j: pl.BlockSpec((tm, D_CONF), lambda i: (rt(i), j))
    prev = lambda j: pl.BlockSpec((HALO, D_CONF), lambda i: (jnp.maximum(rt(i) * per - 1, 0), j))
    nxt = lambda j: pl.BlockSpec((HALO, D_CONF), lambda i: (jnp.minimum((rt(i) + 1) * per, n_halo_blocks - 1), j))
    full = lambda x: pl.BlockSpec(x.shape, lambda i: (0, 0))
    acc = lambda r, w: pl.BlockSpec((r, w), lambda i: (0, 0))
    return pl.pallas_call(
        body, name=name, grid=(nt,),
        in_specs=[cur(0), cur(1), nxt(0), nxt(1), cur(0), nxt(0), cur(0),
                  cur(0), cur(1), prev(0), prev(1), cur(2), nxt(2), cur(3), cur(4), prev(3), prev(4),
                  pl.BlockSpec((tm, N_F_PAD), lambda i: (rt(i), F_BLK)),
                  pl.BlockSpec((tm, N_F_PAD), lambda i: (rt(i), 0)),
                  full(wdw), full(vec), full(wpw), full(wsc)],
        out_specs=[pl.BlockSpec((tm, N_REST), lambda i: (rt(i), 0)), acc(32, D_CONF), acc(8, D_CONF),
                   acc(D_CONF, D_CONF), acc(8, D_CONF)],
        out_shape=[SDS((lp, N_REST), BF16), SDS((32, D_CONF), F32), SDS((8, D_CONF), F32),
                   SDS((D_CONF, D_CONF), F32), SDS((8, D_CONF), F32)],
        scratch_shapes=[pltpu.VMEM((8, N_F_PAD), F32)],
        compiler_params=_cparams(("arbitrary",), 48),
    )(dcs, dcs, dcs, dcs, dwc, dwc, cv, rest, rest, rest, rest, rest, rest, rest, rest, rest, rest, rest, dccol,
      wdw, vec, wpw, wsc)


def _causal_tiles(n, key_major):
    if key_major:
        pairs = [(q, k) for k in range(n) for q in range(k, n)]
    else:
        pairs = [(q, k) for q in range(n) for k in range(q + 1)]
    return (jnp.asarray([p[0] for p in pairs], jnp.int32), jnp.asarray([p[1] for p in pairs], jnp.int32))


def _attn_fwd(qkv, crow, cpair, name):
    lp = qkv.shape[0]
    t = _row_tile(lp)
    nq = lp // t

    qtab, ktab = _causal_tiles(nq, key_major=False)

    def body(qt_ref, kt_ref, q_ref, k_ref, v_ref, c_ref, cq_ref, o_ref, lse_ref, m_scr, l_scr, acc_scr):
        qi, ki = qt_ref[pl.program_id(1)], kt_ref[pl.program_id(1)]

        @pl.when(ki == 0)
        def _():
            lane = lax.broadcasted_iota(jnp.int32, (t, PAIR_W), 1)
            m_scr[...] = jnp.full_like(m_scr, NEG)
            l_scr[...] = jnp.where(lane < 2, 0.0, 1.0)
            acc_scr[...] = jnp.zeros_like(acc_scr)

        def step(diagonal):
            q, k, v = q_ref[...], k_ref[...], v_ref[...]
            lane = lax.broadcasted_iota(jnp.int32, (t, PAIR_W), 1)
            lo = lane < HEAD_DIM
            m_all, l_all = m_scr[...], l_scr[...]
            pv, alpha, m_out, l_out = [], [], [], []
            for j in range(2):
                sel = lo if j == 0 else jnp.logical_not(lo)
                s = _dot_nt(jnp.where(sel, q, jnp.zeros_like(q)), k) + (cq_ref[:, j:j + 1] - c_ref[j:j + 1, :])
                if diagonal:
                    causal = (lax.broadcasted_iota(jnp.int32, (t, t), 0) >= lax.broadcasted_iota(jnp.int32, (t, t), 1))
                    s = jnp.where(causal, s, -jnp.inf)
                m_prev = m_all[:, j:j + 1]
                m_new = jnp.maximum(m_prev, jnp.max(s, axis=1, keepdims=True))
                al = jnp.exp(m_prev - m_new)
                p = jnp.exp(s - m_new)
                l_out.append(al * l_all[:, j:j + 1] + jnp.sum(p, axis=1, keepdims=True))
                m_out.append(m_new)
                pv.append(_dot(p.astype(BF16), v))
                alpha.append(al)
            m_scr[...] = jnp.where(lane == 0, m_out[0], jnp.where(lane == 1, m_out[1], m_all))
            l_scr[...] = jnp.where(lane == 0, l_out[0], jnp.where(lane == 1, l_out[1], l_all))
            acc_scr[...] = acc_scr[...] * jnp.where(lo, alpha[0], alpha[1]) + jnp.where(lo, pv[0], pv[1])

        @pl.when(ki < qi)
        def _():
            step(False)

        @pl.when(ki == qi)
        def _():
            step(True)
            lo = lax.broadcasted_iota(jnp.int32, (t, PAIR_W), 1) < HEAD_DIM
            inv = jnp.where(lo, 1.0 / l_scr[:, 0:1], 1.0 / l_scr[:, 1:2])
            o_ref[...] = (acc_scr[...] * inv).astype(BF16)
            lse_ref[...] = m_scr[...] + jnp.log(l_scr[...])

    kv = lambda off: pl.BlockSpec((t, PAIR_W), lambda p, s, qt, kt: (kt[s], off + p))
    qblk = pl.BlockSpec((t, PAIR_W), lambda p, s, qt, kt: (qt[s], p))
    return pl.pallas_call(
        body, name=name,
        grid_spec=pltpu.PrefetchScalarGridSpec(
            num_scalar_prefetch=2, grid=(N_PAIRS, qtab.shape[0]),
            in_specs=[qblk, kv(N_PAIRS), kv(2 * N_PAIRS),
                      pl.BlockSpec((None, 2, t), lambda p, s, qt, kt: (p, 0, kt[s])), qblk],
            out_specs=[qblk, qblk],
            scratch_shapes=[pltpu.VMEM((t, PAIR_W), F32), pltpu.VMEM((t, PAIR_W), F32), pltpu.VMEM((t, PAIR_W), F32)]),
        out_shape=[SDS((lp, D_ATTN), BF16), SDS((lp, D_ATTN), F32)],
        compiler_params=_cparams(("parallel", "arbitrary"), 48),
    )(qtab, ktab, qkv, qkv, qkv, crow, cpair)


def _head_probs(qj, k, cq, ck, lse, diagonal):
    t = qj.shape[0]
    p = jnp.exp(_dot_nt(qj, k) + (cq - ck) - lse)
    if diagonal:
        causal = (lax.broadcasted_iota(jnp.int32, (t, t), 0) >= lax.broadcasted_iota(jnp.int32, (t, t), 1))
        p = jnp.where(causal, p, 0.0)
    return p


def _attn_bwd(qkv, do, crow, cpair, lse, delta, name):
    lp = qkv.shape[0]
    t = _row_tile(lp)
    nq = lp // t
    qtab, ktab = _causal_tiles(nq, key_major=True)

    def body(qt_ref, kt_ref, q_ref, k_ref, v_ref, do_ref, c_ref, cq_ref, lse_ref, dl_ref,
             dq_ref, dk_ref, dv_ref, dc_ref, dcq_ref, dq_scr, dcq_scr, dk_scr, dv_scr, dc_scr):
        step_id = pl.program_id(1)
        qi, ki = qt_ref[step_id], kt_ref[step_id]

        @pl.when(step_id == 0)
        def _():
            dq_scr[...] = jnp.zeros_like(dq_scr)
            dcq_scr[...] = jnp.zeros_like(dcq_scr)

        @pl.when(qi == ki)
        def _():
            dk_scr[...] = jnp.zeros_like(dk_scr)
            dv_scr[...] = jnp.zeros_like(dv_scr)
            dc_scr[...] = jnp.zeros_like(dc_scr)

        def step(diagonal):
            q, k, v, dout = q_ref[...], k_ref[...], v_ref[...], do_ref[...]
            lane = lax.broadcasted_iota(jnp.int32, (t, PAIR_W), 1)
            lo = lane < HEAD_DIM
            zero = jnp.zeros_like(q)
            dq_new = jnp.zeros((t, PAIR_W), F32)
            dv_new = jnp.zeros((t, PAIR_W), F32)
            dk_new = jnp.zeros((t, PAIR_W), F32)
            col_sums, row_sums = [], []
            for j in range(2):
                sel = lo if j == 0 else jnp.logical_not(lo)
                qj, kj, doj = jnp.where(sel, q, zero), jnp.where(sel, k, zero), jnp.where(sel, dout, zero)
                p = _head_probs(qj, k, cq_ref[:, j:j + 1], c_ref[j:j + 1, :], lse_ref[:, j:j + 1], diagonal)
                dp = _dot_nt(doj, v)
                ds = p * (dp - dl_ref[:, j:j + 1])
                col_sums.append(_colsum(ds))
                row_sums.append(jnp.sum(ds, axis=1, keepdims=True))
                dsb = ds.astype(BF16)
                dv_new = dv_new + _dot_tn(p.astype(BF16), doj)
                dk_new = dk_new + _dot_tn(dsb, qj)
                dq_new = dq_new + _dot(dsb, kj)
            dv_scr[...] += dv_new
            dk_scr[...] += dk_new
            dc_scr[0:2, :] -= jnp.concatenate(col_sums, axis=0)
            rows = pl.ds(pl.multiple_of(qi * t, t), t)
            dq_scr[rows, :] += dq_new
            dcq_scr[rows, :] += jnp.where(lane == 0, row_sums[0], jnp.where(lane == 1, row_sums[1], 0.0))

        @pl.when(qi > ki)
        def _():
            step(False)

        @pl.when(qi == ki)
        def _():
            step(True)
            rows = pl.ds(pl.multiple_of(qi * t, t), t)
            dq_ref[...] = (dq_scr[rows, :] * (HEAD_DIM ** -0.5)).astype(BF16)
            dcq_ref[...] = dcq_scr[rows, :]

        @pl.when(qi == nq - 1)
        def _():
            dk_ref[...] = dk_scr[...].astype(BF16)
            dv_ref[...] = dv_scr[...].astype(BF16)
            dc_ref[...] = dc_scr[0:2, :]

    qside = lambda off: pl.BlockSpec((t, PAIR_W), lambda p, s, qt, kt: (qt[s], off + p))
    kside = lambda off: pl.BlockSpec((t, PAIR_W), lambda p, s, qt, kt: (kt[s], off + p))
    crow_blk = pl.BlockSpec((None, 2, t), lambda p, s, qt, kt: (p, 0, kt[s]))
    return pl.pallas_call(
        body, name=name,
        grid_spec=pltpu.PrefetchScalarGridSpec(
            num_scalar_prefetch=2, grid=(N_PAIRS, qtab.shape[0]),
            in_specs=[qside(0), kside(N_PAIRS), kside(2 * N_PAIRS), qside(0), crow_blk, qside(0), qside(0), qside(0)],
            out_specs=[kside(0), kside(0), kside(0), crow_blk, kside(0)],
            scratch_shapes=[pltpu.VMEM((lp, PAIR_W), F32), pltpu.VMEM((lp, PAIR_W), F32), pltpu.VMEM((t, PAIR_W), F32),
                            pltpu.VMEM((t, PAIR_W), F32), pltpu.VMEM((8, t), F32)]),
        out_shape=[SDS((lp, D_ATTN), BF16), SDS((lp, D_ATTN), BF16), SDS((lp, D_ATTN), BF16),
                   SDS((N_PAIRS, 2, lp), F32), SDS((lp, D_ATTN), F32)],
        compiler_params=_cparams(("parallel", "arbitrary"), 48),
    )(qtab, ktab, qkv, qkv, qkv, do, crow, cpair, lse, delta)


def _shard_sum(own, parts, name, slot_order=False):
    rows, cols = own.shape
    tr = rows if rows <= 512 else 512
    assert rows % tr == 0

    def body(me_ref, own_ref, parts_ref, o_ref):
        if slot_order:
            acc = parts_ref[0].astype(F32)
            for j in range(1, 4):
                acc = acc + parts_ref[j].astype(F32)
        else:
            me = me_ref[0]
            acc = own_ref[...]
            for j in range(4):
                acc = acc + jnp.where(me == j, 0.0, parts_ref[j].astype(F32))
        o_ref[...] = acc

    me = (2 * lax.axis_index("x") + lax.axis_index("y")).astype(jnp.int32).reshape(1)
    return pl.pallas_call(
        body, name=name,
        grid_spec=pltpu.PrefetchScalarGridSpec(
            num_scalar_prefetch=1, grid=(rows // tr,),
            in_specs=[pl.BlockSpec((tr, cols), lambda i, me: (i, 0)), pl.BlockSpec((4, tr, cols), lambda i, me: (0, i, 0))],
            out_specs=pl.BlockSpec((tr, cols), lambda i, me: (i, 0))),
        out_shape=SDS((rows, cols), F32),
        compiler_params=_cparams(("parallel",), 48),
    )(me, own, parts)


def _adamw(w, m, v, s_own, s_sib, name):
    rows, cols = w.shape
    tr = rows if rows <= 256 else 256
    assert rows % tr == 0

    def body(w_ref, m_ref, v_ref, a_ref, b_ref, g_ref, d_ref, nm_ref, nv_ref):
        g = a_ref[...] + b_ref[...]
        g_ref[...] = g
        m = ADAM_B1 * m_ref[...] + (1.0 - ADAM_B1) * g
        v = ADAM_B2 * v_ref[...] + (1.0 - ADAM_B2) * jnp.square(g)
        nm_ref[...] = m
        nv_ref[...] = v
        m_hat = m / (1.0 - ADAM_B1 ** ADAM_STEP)
        v_hat = v / (1.0 - ADAM_B2 ** ADAM_STEP)
        d_ref[...] = -ADAM_LR * (m_hat / (jnp.sqrt(v_hat) + ADAM_EPS) + ADAM_WD * w_ref[...])

    blk = pl.BlockSpec((tr, cols), lambda i: (i, 0))
    return pl.pallas_call(
        body, name=name, grid=(rows // tr,), in_specs=[blk] * 5, out_specs=[blk] * 4,
        out_shape=[SDS((rows, cols), F32)] * 4,
        compiler_params=_cparams(("parallel",), 48),
    )(w, m, v, s_own, s_sib)


def _pad_rows(a, rows):
    return jnp.pad(a, ((0, rows - a.shape[0]), (0, 0)))


def _flat_rows(arrs, cols, rows):
    flat = jnp.concatenate([a.reshape(-1) for a in arrs])
    return jnp.pad(flat, (0, rows * cols - flat.shape[0])).reshape(rows, cols)


def _unflat(buf, shapes):
    flat, out, off = buf.reshape(-1), [], 0
    for s in shapes:
        n = 1
        for d in s:
            n *= d
        out.append(flat[off:off + n].reshape(s))
        off += n
    return out


def kernel(x, meta_tokens, mix_norm_g, w_in, b_forget, w_conf_dw, b_conf_dw, conf_ln_g, conf_ln_b, w_conf_pw, b_conf_pw, w_sc_conv, w_out, mlp_norm_g, w_mlp1, w_mlp2, final_norm_g, loss_target, m_meta_tokens, m_mix_norm_g, m_w_in, m_b_forget, m_w_conf_dw, m_b_conf_dw, m_conf_ln_g, m_conf_ln_b, m_w_conf_pw, m_b_conf_pw, m_w_sc_conv, m_w_out, m_mlp_norm_g, m_w_mlp1, m_w_mlp2, m_final_norm_g, v_meta_tokens, v_mix_norm_g, v_w_in, v_b_forget, v_w_conf_dw, v_b_conf_dw, v_conf_ln_g, v_conf_ln_b, v_w_conf_pw, v_b_conf_pw, v_w_sc_conv, v_w_out, v_mlp_norm_g, v_w_mlp1, v_w_mlp2, v_final_norm_g):
    depth = w_in.shape[0]
    seq = x.shape[1]
    l_tok = N_META + seq
    lp = _padded_len(l_tok)
    n_in = w_in.shape[2] * 4
    n_in_sh = w_in.shape[2]
    me = 2 * lax.axis_index("x") + lax.axis_index("y")

    small_shapes = [w_conf_dw.shape, w_conf_pw.shape, w_sc_conv.shape, meta_tokens.shape]
    n_small = sum(a.size for a in (w_conf_dw, w_conf_pw, w_sc_conv, meta_tokens))
    small_rows = -(-n_small // 1024 // 8) * 8
    small = _flat_rows([w_conf_dw, w_conf_pw, w_sc_conv, meta_tokens], 1024, small_rows)
    g_in, g_out, g_m1, g_m2, g_small = _chip_exchange(
        [w_in.astype(BF16).reshape(depth * D_MODEL, n_in_sh), w_out.astype(BF16).reshape(-1, D_MODEL),
         w_mlp1.astype(BF16).reshape(depth * D_MODEL, -1), w_mlp2.astype(BF16).reshape(-1, D_MODEL), small],
        gather=True, name="weights_all_gather")
    w_in_f = jnp.concatenate([g_in[j].reshape(depth, D_MODEL, n_in_sh) for j in range(4)], axis=2)
    w_out_f = jnp.concatenate([g_out[j].reshape(depth, -1, D_MODEL) for j in range(4)], axis=1)
    w_m1_f = jnp.concatenate([g_m1[j].reshape(depth, D_MODEL, -1) for j in range(4)], axis=2)
    w_m2_f = jnp.concatenate([g_m2[j].reshape(depth, -1, D_MODEL) for j in range(4)], axis=1)
    sm = [_unflat(g_small[j], small_shapes) for j in range(4)]
    w_dw_f = jnp.concatenate([s[0] for s in sm], axis=2)
    w_pw_f = jnp.concatenate([s[1] for s in sm], axis=1)
    w_sc_f = jnp.concatenate([s[2] for s in sm], axis=2)
    meta_f = jnp.concatenate([s[3] for s in sm], axis=1)

    q_end, f_end = 3 * D_ATTN, 3 * D_ATTN + N_HEADS
    wqkv = [w_in_f[l, :, :q_end] for l in range(depth)]
    wrest = [jnp.concatenate([w_in_f[l, :, f_end:], w_in_f[l, :, q_end:f_end],
                              jnp.zeros((D_MODEL, N_F_PAD - N_HEADS), BF16)], axis=1) for l in range(depth)]
    wdw = [_pad_rows(w_dw_f[l], 32) for l in range(depth)]
    wsc = [_pad_rows(w_sc_f[l], 8) for l in range(depth)]
    wpw = [w_pw_f[l].astype(BF16) for l in range(depth)]
    vec = [_pad_rows(jnp.stack([b_conf_dw[l], conf_ln_g[l], conf_ln_b[l], b_conf_pw[l],
                                jnp.pad(b_forget[l], (0, D_CONF - N_HEADS))]), 8) for l in range(depth)]
    col = jnp.arange(D_ATTN)
    seg = (col[None, :] == ((col // PAIR_W) * PAIR_W + (col % PAIR_W) // HEAD_DIM)[:, None]).astype(F32)

    h = jnp.concatenate([meta_f, x[0], jnp.zeros((lp - l_tok, D_MODEL), F32)], axis=0)
    target = jnp.pad(loss_target[0], ((N_META, lp - l_tok), (0, 0)))
    saved = []
    for l in range(depth):
        g1, g2 = mix_norm_g[l][None, :], mlp_norm_g[l][None, :]
        hn1, qkv, rest = _proj_fwd(h, g1, wqkv[l], wrest[l], name=f"proj_fwd_{l}")
        cs, dwc, cv, ccol = _conv_fwd(rest, wdw[l], vec[l], wpw[l], wsc[l], name=f"conv_fwd_{l}")
        crow = ccol[:, :N_HEADS].T.reshape(N_PAIRS, 2, lp)
        cpair = jnp.pad(ccol[:, :N_HEADS].reshape(lp, N_PAIRS, 2), ((0, 0), (0, 0), (0, PAIR_W - 2))).reshape(lp, D_ATTN)
        attn, lse = _attn_fwd(qkv, crow, cpair, name=f"attn_fwd_{l}")
        h1 = _mix_out(h, attn, cs, w_out_f[l, :D_ATTN], w_out_f[l, D_ATTN:], name=f"mix_out_{l}")
        hn2, a, r, h2 = _mlp_fwd(h1, g2, w_m1_f[l], w_m2_f[l], name=f"mlp_fwd_{l}")
        saved.append((h, g1, g2, hn1, qkv, rest, cs, dwc, cv, crow, cpair, attn, lse, h1, hn2, a, r))
        h = h2
    dh, loss11, d_final_g = _loss_head(h, target, final_norm_g[None, :], seq, name="loss_head")
    loss = lax.psum(loss11[0, 0], ("x", "y", "c"))

    grads = [None] * depth
    for l in reversed(range(depth)):
        h0, g1, g2, hn1, qkv, rest, cs, dwc, cv, crow, cpair, attn, lse, h1, hn2, a, r = saved[l]
        da, dh1, dg2 = _mlp_bwd(dh, h1, g2, a, w_m1_f[l], w_m2_f[l], name=f"mlp_bwd_{l}")
        d_w2 = _mm_tn(r, dh, name=f"dw_mlp2_{l}")
        d_w1 = _mm_tn(hn2, da, name=f"dw_mlp1_{l}")
        do, dcs, delta = _mix_bwd(dh1, w_out_f[l, :D_ATTN], w_out_f[l, D_ATTN:], attn, seg, name=f"mix_bwd_{l}")
        d_wout = jnp.concatenate([_mm_tn(attn, dh1, name=f"dw_out_a_{l}"), _mm_tn(cs, dh1, name=f"dw_out_c_{l}")], axis=0)
        dq, dk, dv, dcrow, dcq = _attn_bwd(qkv, do, crow, cpair, lse, delta, name=f"attn_bwd_{l}")
        dc_heads = dcrow.reshape(N_HEADS, lp).T + dcq.reshape(lp, N_PAIRS, PAIR_W)[:, :, :2].reshape(lp, N_HEADS)
        dccol = jnp.pad(dc_heads, ((0, 0), (0, N_F_PAD - N_HEADS)))
        drest, d_wdw, d_wsc, d_wpw, d_vec = _conv_bwd(dcs, dwc, cv, rest, dccol, wdw[l], vec[l], wpw[l], wsc[l],
                                                      name=f"conv_bwd_{l}")
        dh, dg1 = _in_bwd(dh1, h0, g1, dq, dk, dv, drest, wqkv[l], wrest[l], name=f"in_bwd_{l}")
        d_wq = _mm_tn(hn1, dq, name=f"dw_q_{l}")
        d_wk = _mm_tn(hn1, dk, name=f"dw_k_{l}")
        d_wv = _mm_tn(hn1, dv, name=f"dw_v_{l}")
        d_wr = _mm_tn(hn1, drest, name=f"dw_rest_{l}")
        d_win = jnp.concatenate([d_wq, d_wk, d_wv, d_wr[:, 5 * D_CONF:5 * D_CONF + N_HEADS], d_wr[:, :5 * D_CONF]], axis=1)
        grads[l] = dict(w_in=d_win, w_out=d_wout, w_mlp1=d_w1, w_mlp2=d_w2, mix_norm_g=dg1[0], mlp_norm_g=dg2[0],
                        w_conf_dw=d_wdw[:CONF_K], w_sc_conv=d_wsc[:SC_K], w_conf_pw=d_wpw, b_conf_dw=d_vec[0],
                        conf_ln_g=d_vec[1], conf_ln_b=d_vec[2], b_conf_pw=d_vec[3], b_forget=d_vec[4, :N_HEADS])
    grad_x = dh[N_META:N_META + seq][None]
    d_meta = dh[:N_META]

    def stack(k):
        return jnp.stack([grads[l][k] for l in range(depth)])

    def col_shards(a):
        d, rr, cc = a.shape
        return a.reshape(d, rr, 4, cc // 4).transpose(2, 0, 1, 3).reshape(4, d * rr, cc // 4)

    def row_shards(a):
        d, rr, cc = a.shape
        return a.reshape(d, 4, rr // 4, cc).transpose(1, 0, 2, 3).reshape(4, d * rr // 4, cc)

    big = [col_shards(stack("w_in")), row_shards(stack("w_out")), col_shards(stack("w_mlp1")), row_shards(stack("w_mlp2"))]
    rep = [stack("mix_norm_g"), stack("b_forget"), stack("b_conf_dw"), stack("conf_ln_g"), stack("conf_ln_b"),
           stack("b_conf_pw"), stack("mlp_norm_g"), d_final_g[0]]
    rep_shapes = [a.shape for a in rep]
    sh_dw, sh_pw, sh_sc = col_shards(stack("w_conf_dw")), row_shards(stack("w_conf_pw")), col_shards(stack("w_sc_conv"))
    sh_meta = d_meta.reshape(N_META, 4, D_MODEL // 4).transpose(1, 0, 2)
    n_rep = sum(a.size for a in rep)
    gs_rows = -(-(n_small + n_rep) // 1024 // 8) * 8
    gsmall = jnp.stack([_flat_rows([sh_dw[j], sh_pw[j], sh_sc[j], sh_meta[j]] + rep, 1024, gs_rows) for j in range(4)])
    recv = _chip_exchange([b.astype(BF16) for b in big] + [gsmall], gather=False, name="grads_exchange")
    own = [lax.dynamic_index_in_dim(b, me, 0, keepdims=False) for b in big + [gsmall]]
    sums = [_shard_sum(own[i], recv[i], name=f"shard_sum_{i}", slot_order=(i == 4)) for i in range(5)]
    sibs = _sibling_exchange(sums, name="sibling_exchange")

    w_small = _flat_rows([w_conf_dw, w_conf_pw, w_sc_conv, meta_tokens, mix_norm_g, b_forget, b_conf_dw, conf_ln_g,
                          conf_ln_b, b_conf_pw, mlp_norm_g, final_norm_g], 1024, gs_rows)
    m_small = _flat_rows([m_w_conf_dw, m_w_conf_pw, m_w_sc_conv, m_meta_tokens, m_mix_norm_g, m_b_forget, m_b_conf_dw,
                          m_conf_ln_g, m_conf_ln_b, m_b_conf_pw, m_mlp_norm_g, m_final_norm_g], 1024, gs_rows)
    v_small = _flat_rows([v_w_conf_dw, v_w_conf_pw, v_w_sc_conv, v_meta_tokens, v_mix_norm_g, v_b_forget, v_b_conf_dw,
                          v_conf_ln_g, v_conf_ln_b, v_b_conf_pw, v_mlp_norm_g, v_final_norm_g], 1024, gs_rows)
    triples = [(w_in, m_w_in, v_w_in), (w_out, m_w_out, v_w_out), (w_mlp1, m_w_mlp1, v_w_mlp1),
               (w_mlp2, m_w_mlp2, v_w_mlp2), (w_small, m_small, v_small)]
    res = []
    for i, (w, m, v) in enumerate(triples):
        shp = w.shape
        two_d = sums[i].shape
        out4 = _adamw(w.reshape(two_d), m.reshape(two_d), v.reshape(two_d), sums[i], sibs[i], name=f"adamw_{i}")
        res.append([o.reshape(shp) for o in out4])
    small_out = [_unflat(res[4][k], small_shapes + rep_shapes) for k in range(4)]

    names = ["meta_tokens", "mix_norm_g", "w_in", "b_forget", "w_conf_dw", "b_conf_dw", "conf_ln_g", "conf_ln_b",
             "w_conf_pw", "b_conf_pw", "w_sc_conv", "w_out", "mlp_norm_g", "w_mlp1", "w_mlp2", "final_norm_g"]
    small_order = ["w_conf_dw", "w_conf_pw", "w_sc_conv", "meta_tokens", "mix_norm_g", "b_forget", "b_conf_dw",
                   "conf_ln_g", "conf_ln_b", "b_conf_pw", "mlp_norm_g", "final_norm_g"]
    big_order = {"w_in": 0, "w_out": 1, "w_mlp1": 2, "w_mlp2": 3}
    outs = []
    for k in range(4):
        for nme in names:
            outs.append(res[big_order[nme]][k] if nme in big_order else small_out[k][small_order.index(nme)])
    return (loss, grad_x, *outs)
```

```python
import jax
import jax.numpy as jnp
from jax import lax
from jax.experimental import pallas as pl
from jax.experimental.pallas import tpu as pltpu

F32, BF16 = jnp.float32, jnp.bfloat16
D_MODEL = 1024
D_ATTN = 512
D_CONF = 256
N_HEADS = 8
HEAD_DIM = 64
N_PAIRS = N_HEADS // 2
PAIR_W = 2 * HEAD_DIM
N_META = 16
D_FF = 4096
CONF_K = 31
SC_K = 3
HALO = 32
N_F_PAD = 128
N_REST = 5 * D_CONF + N_F_PAD
F_BLK = 5 * D_CONF // N_F_PAD
EPS = 1e-6
NEG = -1e30
BIG_TILE = 640
SMALL_TILE = 128
MLP_TILE_DIV = 2
ATTN_CHUNK = 32
V7X_VMEM_BYTES = 64 * 1024 * 1024
ADAM_LR, ADAM_B1, ADAM_B2, ADAM_EPS, ADAM_WD, ADAM_STEP = 0.001, 0.9, 0.999, 1e-08, 0.01, 10
MESH = pl.DeviceIdType.MESH
HIGHEST = lax.Precision.HIGHEST
SDS = jax.ShapeDtypeStruct


def _cparams(sem=None, vmem_mb=None):
    kw = {}
    if sem is not None:
        kw["dimension_semantics"] = sem
    if vmem_mb is not None:
        assert vmem_mb * 1024 * 1024 <= V7X_VMEM_BYTES
        kw["vmem_limit_bytes"] = vmem_mb * 1024 * 1024
    return pltpu.CompilerParams(**kw)


def _dot(a, b):
    return jnp.dot(a, b, preferred_element_type=F32)


def _dot_nt(a, b):
    return lax.dot_general(a, b, (((1,), (1,)), ((), ())), preferred_element_type=F32)


def _dot_tn(a, b):
    return lax.dot_general(a, b, (((0,), (0,)), ((), ())), preferred_element_type=F32)


def _sigmoid(x):
    return 1.0 / (1.0 + jnp.exp(-x))


def _rms_fwd(x, g):
    r = lax.rsqrt(jnp.mean(x * x, axis=-1, keepdims=True) + EPS)
    xn = x * r
    return r, xn, xn * g


def _rms_bwd(r, xn, g, dy):
    u = dy * g
    return r * (u - xn * jnp.mean(u * xn, axis=-1, keepdims=True))


def _colsum(x):
    return jnp.sum(x, axis=0, keepdims=True)


def _row_tile(lp):
    return BIG_TILE if lp % BIG_TILE == 0 else SMALL_TILE


def _padded_len(l):
    t = BIG_TILE if l >= BIG_TILE else SMALL_TILE
    return -(-l // t) * t


def _chip_exchange(arrs, gather, name):
    n = len(arrs)

    def body(*refs):
        ins, outs = refs[:n], refs[n:2 * n]
        send_sems, recv_sems, local_sems = refs[2 * n:]
        x, y, c = lax.axis_index("x"), lax.axis_index("y"), lax.axis_index("c")
        me = 2 * x + y
        chips = [(1 - x, y), (x, 1 - y), (1 - x, 1 - y)]
        local, sends, recvs = [], [], []
        for i in range(n):
            src_me = ins[i] if gather else ins[i].at[me]
            local.append(pltpu.make_async_copy(src_me, outs[i].at[me], local_sems.at[i]))
            for j, (px, py) in enumerate(chips):
                k = 3 * i + j
                peer = 2 * px + py
                src = ins[i] if gather else ins[i].at[peer]
                sends.append(pltpu.make_async_remote_copy(
                    src_ref=src, dst_ref=outs[i].at[me], send_sem=send_sems.at[k], recv_sem=recv_sems.at[k],
                    device_id=(px, py, c), device_id_type=MESH))
                recvs.append(pltpu.make_async_remote_copy(
                    src_ref=src, dst_ref=outs[i].at[peer], send_sem=send_sems.at[k], recv_sem=recv_sems.at[k],
                    device_id=(px, py, c), device_id_type=MESH))
        for cp in local + sends:
            cp.start()
        for cp in recvs:
            cp.wait_recv()
        for cp in sends:
            cp.wait_send()
        for cp in local:
            cp.wait()

    any_spec = pl.BlockSpec(memory_space=pl.ANY)
    out_shape = [SDS((4,) + a.shape, a.dtype) if gather else SDS(a.shape, a.dtype) for a in arrs]
    return pl.pallas_call(
        body, name=name, out_shape=out_shape,
        in_specs=[any_spec] * n, out_specs=[any_spec] * n,
        scratch_shapes=[pltpu.SemaphoreType.DMA((3 * n,)), pltpu.SemaphoreType.DMA((3 * n,)),
                        pltpu.SemaphoreType.DMA((n,))],
    )(*arrs)


def _sibling_exchange(arrs, name):
    n = len(arrs)

    def body(*refs):
        ins, outs = refs[:n], refs[n:2 * n]
        send_sems, recv_sems = refs[2 * n:]
        sib = (lax.axis_index("x"), lax.axis_index("y"), 1 - lax.axis_index("c"))
        cps = [pltpu.make_async_remote_copy(src_ref=ins[i], dst_ref=outs[i], send_sem=send_sems.at[i],
                                            recv_sem=recv_sems.at[i], device_id=sib, device_id_type=MESH)
               for i in range(n)]
        for cp in cps:
            cp.start()
        for cp in cps:
            cp.wait_recv()
        for cp in cps:
            cp.wait_send()

    any_spec = pl.BlockSpec(memory_space=pl.ANY)
    return pl.pallas_call(
        body, name=name, out_shape=[SDS(a.shape, a.dtype) for a in arrs],
        in_specs=[any_spec] * n, out_specs=[any_spec] * n,
        scratch_shapes=[pltpu.SemaphoreType.DMA((n,)), pltpu.SemaphoreType.DMA((n,))],
    )(*arrs)


def _proj_fwd(h, g, wqkv, wrest, name):
    lp = h.shape[0]
    tm = _row_tile(lp)

    def body(h_ref, g_ref, wq_ref, wr_ref, hn_ref, qkv_ref, rest_ref):
        _, _, y = _rms_fwd(h_ref[...], g_ref[...])
        hn = y.astype(BF16)
        hn_ref[...] = hn
        qkv = _dot(hn, wq_ref[...])
        qkv_ref[:, :D_ATTN] = (qkv[:, :D_ATTN] * (HEAD_DIM ** -0.5)).astype(BF16)
        qkv_ref[:, D_ATTN:] = qkv[:, D_ATTN:].astype(BF16)
        rest_ref[...] = _dot(hn, wr_ref[...])

    row = lambda w: pl.BlockSpec((tm, w), lambda i: (i, 0))
    full = lambda a: pl.BlockSpec(a.shape, lambda i: (0, 0))
    return pl.pallas_call(
        body, name=name, grid=(lp // tm,),
        in_specs=[row(D_MODEL), full(g), full(wqkv), full(wrest)],
        out_specs=[row(D_MODEL), row(3 * D_ATTN), row(N_REST)],
        out_shape=[SDS((lp, D_MODEL), BF16), SDS((lp, 3 * D_ATTN), BF16), SDS((lp, N_REST), F32)],
        compiler_params=_cparams(("parallel",), 48),
    )(h, g, wqkv, wrest)


def _mm_tn(a, b, name):
    kk, m = a.shape
    n = b.shape[1]
    tk = _row_tile(kk)
    tmo = min(m, 1024)
    tno = n if n <= 1536 else 1024

    def body(a_ref, b_ref, o_ref):
        @pl.when(pl.program_id(2) == 0)
        def _():
            o_ref[...] = jnp.zeros_like(o_ref)
        o_ref[...] += _dot_tn(a_ref[...], b_ref[...].astype(BF16))

    return pl.pallas_call(
        body, name=name, grid=(m // tmo, n // tno, kk // tk),
        in_specs=[pl.BlockSpec((tk, tmo), lambda i, j, k: (k, i)), pl.BlockSpec((tk, tno), lambda i, j, k: (k, j))],
        out_specs=pl.BlockSpec((tmo, tno), lambda i, j, k: (i, j)),
        out_shape=SDS((m, n), F32),
        compiler_params=_cparams(("parallel", "parallel", "arbitrary"), 48),
    )(a, b)


def _mix_out(h, attn, cs, wa, wc, name):
    lp = h.shape[0]
    tm = _row_tile(lp)

    def body(h_ref, at_ref, cs_ref, wa_ref, wc_ref, o_ref):
        o_ref[...] = h_ref[...] + _dot(at_ref[...], wa_ref[...]) + _dot(cs_ref[...], wc_ref[...])

    row = lambda w: pl.BlockSpec((tm, w), lambda i: (i, 0))
    full = lambda a: pl.BlockSpec(a.shape, lambda i: (0, 0))
    return pl.pallas_call(
        body, name=name, grid=(lp // tm,),
        in_specs=[row(D_MODEL), row(D_ATTN), row(D_ATTN), full(wa), full(wc)],
        out_specs=row(D_MODEL), out_shape=SDS((lp, D_MODEL), F32),
        compiler_params=_cparams(("parallel",), 48),
    )(h, attn, cs, wa, wc)


def _mlp_fwd(h, g, w1, w2, name):
    lp = h.shape[0]
    tm = _row_tile(lp) // MLP_TILE_DIV

    def body(h_ref, g_ref, w1_ref, w2_ref, hn_ref, a_ref, r_ref, o_ref):
        x = h_ref[...]
        _, _, y = _rms_fwd(x, g_ref[...])
        hn = y.astype(BF16)
        hn_ref[...] = hn
        a = _dot(hn, w1_ref[...])
        a_ref[...] = a
        r = jnp.square(jnp.maximum(a, 0.0)).astype(BF16)
        r_ref[...] = r
        o_ref[...] = x + _dot(r, w2_ref[...])

    row = lambda w: pl.BlockSpec((tm, w), lambda i: (i, 0))
    vmem = pl.BlockSpec(memory_space=pltpu.VMEM)
    return pl.pallas_call(
        body, name=name, grid=(lp // tm,),
        in_specs=[row(D_MODEL), pl.BlockSpec(g.shape, lambda i: (0, 0)), vmem, vmem],
        out_specs=[row(D_MODEL), row(D_FF), row(D_FF), row(D_MODEL)],
        out_shape=[SDS((lp, D_MODEL), BF16), SDS((lp, D_FF), F32), SDS((lp, D_FF), BF16), SDS((lp, D_MODEL), F32)],
        compiler_params=_cparams(("parallel",), 60),
    )(h, g, w1, w2)


def _mlp_bwd(dh2, h1, g, a, w1, w2, name):
    lp = h1.shape[0]
    tm = _row_tile(lp) // MLP_TILE_DIV

    def body(dy_ref, h_ref, g_ref, a_ref, w1_ref, w2_ref, da_ref, dh_ref, dg_ref):
        @pl.when(pl.program_id(0) == 0)
        def _():
            dg_ref[...] = jnp.zeros_like(dg_ref)
        dy = dy_ref[...]
        dr = _dot_nt(dy.astype(BF16), w2_ref[...])
        da = (dr * (2.0 * jnp.maximum(a_ref[...], 0.0))).astype(BF16)
        da_ref[...] = da
        dhn = _dot_nt(da, w1_ref[...])
        r, xn, _ = _rms_fwd(h_ref[...], g_ref[...])
        dg_ref[...] += _colsum(dhn * xn)
        dh_ref[...] = dy + _rms_bwd(r, xn, g_ref[...], dhn)

    row = lambda w: pl.BlockSpec((tm, w), lambda i: (i, 0))
    vmem = pl.BlockSpec(memory_space=pltpu.VMEM)
    vec = pl.BlockSpec((1, D_MODEL), lambda i: (0, 0))
    return pl.pallas_call(
        body, name=name, grid=(lp // tm,),
        in_specs=[row(D_MODEL), row(D_MODEL), vec, row(D_FF), vmem, vmem],
        out_specs=[row(D_FF), row(D_MODEL), vec],
        out_shape=[SDS((lp, D_FF), BF16), SDS((lp, D_MODEL), F32), SDS((1, D_MODEL), F32)],
        compiler_params=_cparams(("arbitrary",), 60),
    )(dh2, h1, g, a, w1, w2)


def _mix_bwd(dh1, wa, wc, attn, seg, name):
    lp = dh1.shape[0]
    tm = _row_tile(lp)

    def body(dy_ref, wa_ref, wc_ref, o_ref, seg_ref, do_ref, dcs_ref, dl_ref):
        dy = dy_ref[...].astype(BF16)
        do = _dot_nt(dy, wa_ref[...])
        do_ref[...] = do.astype(BF16)
        dcs_ref[...] = _dot_nt(dy, wc_ref[...])
        dl_ref[...] = jnp.dot(do * o_ref[...].astype(F32), seg_ref[...], precision=HIGHEST,
                              preferred_element_type=F32)

    row = lambda w: pl.BlockSpec((tm, w), lambda i: (i, 0))
    full = lambda x: pl.BlockSpec(x.shape, lambda i: (0, 0))
    return pl.pallas_call(
        body, name=name, grid=(lp // tm,),
        in_specs=[row(D_MODEL), full(wa), full(wc), row(D_ATTN), full(seg)],
        out_specs=[row(D_ATTN), row(D_ATTN), row(D_ATTN)],
        out_shape=[SDS((lp, D_ATTN), BF16), SDS((lp, D_ATTN), F32), SDS((lp, D_ATTN), F32)],
        compiler_params=_cparams(("parallel",), 48),
    )(dh1, wa, wc, attn, seg)


def _in_bwd(dh1, h, g, dq, dk, dv, drest, wqkv, wrest, name):
    lp = h.shape[0]
    tm = _row_tile(lp)

    def body(dy_ref, h_ref, g_ref, dq_ref, dk_ref, dv_ref, dr_ref, wq_ref, wr_ref, dh_ref, dg_ref):
        @pl.when(pl.program_id(0) == 0)
        def _():
            dg_ref[...] = jnp.zeros_like(dg_ref)
        dhn = (_dot_nt(dq_ref[...], wq_ref[:, 0:D_ATTN]) + _dot_nt(dk_ref[...], wq_ref[:, D_ATTN:2 * D_ATTN])
               + _dot_nt(dv_ref[...], wq_ref[:, 2 * D_ATTN:]) + _dot_nt(dr_ref[...], wr_ref[...]))
        r, xn, _ = _rms_fwd(h_ref[...], g_ref[...])
        dg_ref[...] += _colsum(dhn * xn)
        dh_ref[...] = dy_ref[...] + _rms_bwd(r, xn, g_ref[...], dhn)

    row = lambda w: pl.BlockSpec((tm, w), lambda i: (i, 0))
    full = lambda x: pl.BlockSpec(x.shape, lambda i: (0, 0))
    vec = pl.BlockSpec((1, D_MODEL), lambda i: (0, 0))
    return pl.pallas_call(
        body, name=name, grid=(lp // tm,),
        in_specs=[row(D_MODEL), row(D_MODEL), vec, row(D_ATTN), row(D_ATTN), row(D_ATTN), row(N_REST),
                  full(wqkv), full(wrest)],
        out_specs=[row(D_MODEL), vec],
        out_shape=[SDS((lp, D_MODEL), F32), SDS((1, D_MODEL), F32)],
        compiler_params=_cparams(("arbitrary",), 48),
    )(dh1, h, g, dq, dk, dv, drest, wqkv, wrest)


def _loss_head(h, target, g, seq, name):
    lp = h.shape[0]
    tm = _row_tile(lp)

    def body(h_ref, t_ref, g_ref, dh_ref, loss_ref, dg_ref):
        i = pl.program_id(0)

        @pl.when(i == 0)
        def _():
            dg_ref[...] = jnp.zeros_like(dg_ref)
            loss_ref[...] = jnp.zeros_like(loss_ref)
        r, xn, y = _rms_fwd(h_ref[...], g_ref[...])
        rows = i * tm + lax.broadcasted_iota(jnp.int32, (tm, 1), 0)
        real = (rows >= N_META) & (rows < N_META + seq)
        e = jnp.where(real, y - t_ref[...], 0.0)
        loss_ref[...] += 0.5 * jnp.sum(jnp.mean(e * e, axis=-1, keepdims=True), axis=0, keepdims=True)
        dy = e * (1.0 / D_MODEL)
        dg_ref[...] += _colsum(dy * xn)
        dh_ref[...] = _rms_bwd(r, xn, g_ref[...], dy)

    row = pl.BlockSpec((tm, D_MODEL), lambda i: (i, 0))
    vec = pl.BlockSpec((1, D_MODEL), lambda i: (0, 0))
    return pl.pallas_call(
        body, name=name, grid=(lp // tm,),
        in_specs=[row, row, vec],
        out_specs=[row, pl.BlockSpec((1, 1), lambda i: (0, 0)), vec],
        out_shape=[SDS((lp, D_MODEL), F32), SDS((1, 1), F32), SDS((1, D_MODEL), F32)],
        compiler_params=_cparams(("arbitrary",), 48),
    )(h, target, g)


def _shift_down(xcat, j):
    return (xcat if j == 0 else pltpu.roll(xcat, j, 0))[HALO:, :]


def _shift_up(xcat, j, tm):
    n = xcat.shape[0]
    return (xcat if j == 0 else pltpu.roll(xcat, n - j, 0))[:tm, :]


def _layernorm_parts(x):
    mu = jnp.mean(x, axis=-1, keepdims=True)
    xc = x - mu
    rstd = lax.rsqrt(jnp.mean(xc * xc, axis=-1, keepdims=True) + EPS)
    return rstd, xc * rstd


def _log_sigmoid(z):
    return jnp.minimum(z, 0.0) - jnp.log(1.0 + jnp.exp(-jnp.abs(z)))


def _conv_fwd(rest, wdw, vec, wpw, wsc, name):
    lp = rest.shape[0]
    tm = _row_tile(lp)
    per = tm // HALO

    def body(a_ref, g_ref, ah_ref, gh_ref, b_ref, c_ref, u_ref, ch_ref, uh_ref, f_ref,
             wdw_ref, vec_ref, wpw_ref, wsc_ref, cs_ref, dwc_ref, cv_ref, cc_ref, carry):
        i = pl.program_id(0)
        first = i == 0

        @pl.when(first)
        def _():
            carry[...] = jnp.zeros_like(carry)
        glu = a_ref[...] * _sigmoid(g_ref[...])
        glu_h = jnp.where(first, 0.0, ah_ref[...] * _sigmoid(gh_ref[...]))
        xcat = jnp.concatenate([glu_h, glu], axis=0)
        acc = jnp.zeros((tm, D_CONF), F32) + vec_ref[0:1, :]
        for k in range(CONF_K):
            acc = acc + wdw_ref[k:k + 1, :] * _shift_down(xcat, CONF_K - 1 - k)
        dwc_ref[...] = acc
        _, y = _layernorm_parts(acc)
        ln = y * vec_ref[1:2, :] + vec_ref[2:3, :]
        sw = ln * _sigmoid(ln)
        conf = _dot(sw.astype(BF16), wpw_ref[...]) + vec_ref[3:4, :]
        p = c_ref[...] * u_ref[...]
        p_h = jnp.where(first, 0.0, ch_ref[...] * uh_ref[...])
        pcat = jnp.concatenate([p_h, p], axis=0)
        cv = jnp.zeros((tm, D_CONF), F32)
        for k in range(SC_K):
            cv = cv + wsc_ref[k:k + 1, :] * _shift_down(pcat, SC_K - 1 - k)
        cv_ref[...] = cv
        cs_ref[:, :D_CONF] = conf.astype(BF16)
        cs_ref[:, D_CONF:] = (b_ref[...] * cv).astype(BF16)
        logf = _log_sigmoid(f_ref[...] + vec_ref[4:5, :N_F_PAD])
        tri = (lax.broadcasted_iota(jnp.int32, (tm, tm), 0) >= lax.broadcasted_iota(jnp.int32, (tm, tm), 1))
        c = jnp.dot(tri.astype(F32), logf, precision=HIGHEST, preferred_element_type=F32) + carry[0:1, :]
        cc_ref[...] = c
        carry[0:1, :] = c[tm - 1:tm, :]

    cur = lambda j: pl.BlockSpec((tm, D_CONF), lambda i: (i, j))
    prev = lambda j: pl.BlockSpec((HALO, D_CONF), lambda i: (jnp.maximum(i * per - 1, 0), j))
    full = lambda x: pl.BlockSpec(x.shape, lambda i: (0, 0))
    return pl.pallas_call(
        body, name=name, grid=(lp // tm,),
        in_specs=[cur(0), cur(1), prev(0), prev(1), cur(2), cur(3), cur(4), prev(3), prev(4),
                  pl.BlockSpec((tm, N_F_PAD), lambda i: (i, F_BLK)),
                  full(wdw), full(vec), full(wpw), full(wsc)],
        out_specs=[pl.BlockSpec((tm, 2 * D_CONF), lambda i: (i, 0)), cur(0), cur(0),
                   pl.BlockSpec((tm, N_F_PAD), lambda i: (i, 0))],
        out_shape=[SDS((lp, 2 * D_CONF), BF16), SDS((lp, D_CONF), F32), SDS((lp, D_CONF), F32),
                   SDS((lp, N_F_PAD), F32)],
        scratch_shapes=[pltpu.VMEM((8, N_F_PAD), F32)],
        compiler_params=_cparams(("arbitrary",), 48),
    )(rest, rest, rest, rest, rest, rest, rest, rest, rest, rest, wdw, vec, wpw, wsc)


def _conv_bwd(dcs, dwc, cv, rest, dccol, wdw, vec, wpw, wsc, name):
    lp = rest.shape[0]
    tm = _row_tile(lp)
    per = tm // HALO
    nt = lp // tm
    n_halo_blocks = lp // HALO

    def body(dcf_ref, dsc_ref, dcfn_ref, dscn_ref, dwc_ref, dwcn_ref, cv_ref,
             a_ref, g_ref, ah_ref, gh_ref, b_ref, bn_ref, c_ref, u_ref, ch_ref, uh_ref, f_ref, dcc_ref,
             wdw_ref, vec_ref, wpw_ref, wsc_ref,
             dr_ref, dwdw_ref, dwsc_ref, dwpw_ref, dvec_ref, carry):
        i = pl.program_id(0)
        first = i == nt - 1
        last = i == 0

        @pl.when(i == 0)
        def _():
            carry[...] = jnp.zeros_like(carry)
            dwdw_ref[...] = jnp.zeros_like(dwdw_ref)
            dwsc_ref[...] = jnp.zeros_like(dwsc_ref)
            dwpw_ref[...] = jnp.zeros_like(dwpw_ref)
            dvec_ref[...] = jnp.zeros_like(dvec_ref)

        ln_g, ln_b = vec_ref[1:2, :], vec_ref[2:3, :]

        def ln_bwd(dconf, x):
            rstd, y = _layernorm_parts(x)
            ln = y * ln_g + ln_b
            sg = _sigmoid(ln)
            dsw = _dot_nt(dconf.astype(BF16), wpw_ref[...])
            dln = dsw * (sg * (1.0 + ln * (1.0 - sg)))
            dy = dln * ln_g
            ddw = rstd * (dy - jnp.mean(dy, axis=-1, keepdims=True) - y * jnp.mean(dy * y, axis=-1, keepdims=True))
            return ln * sg, y, dln, ddw

        dconf = dcf_ref[...]
        sw, y, dln, ddw = ln_bwd(dconf, dwc_ref[...])
        _, _, _, ddw_n = ln_bwd(dcfn_ref[...], dwcn_ref[...])
        ddw_n = jnp.where(last, 0.0, ddw_n)
        dvec_ref[0:1, :] += _colsum(ddw)
        dvec_ref[1:2, :] += _colsum(dln * y)
        dvec_ref[2:3, :] += _colsum(dln)
        dvec_ref[3:4, :] += _colsum(dconf)
        dwpw_ref[...] += _dot_tn(sw.astype(BF16), dconf.astype(BF16))

        a, sg = a_ref[...], _sigmoid(g_ref[...])
        glu = a * sg
        glu_h = jnp.where(first, 0.0, ah_ref[...] * _sigmoid(gh_ref[...]))
        xcat = jnp.concatenate([glu_h, glu], axis=0)
        dcat = jnp.concatenate([ddw, ddw_n], axis=0)
        dglu = jnp.zeros((tm, D_CONF), F32)
        for k in range(CONF_K):
            j = CONF_K - 1 - k
            dwdw_ref[k:k + 1, :] += _colsum(ddw * _shift_down(xcat, j))
            dglu = dglu + wdw_ref[k:k + 1, :] * _shift_up(dcat, j, tm)
        dr_ref[:, 0:D_CONF] = (dglu * sg).astype(BF16)
        dr_ref[:, D_CONF:2 * D_CONF] = (dglu * a * sg * (1.0 - sg)).astype(BF16)

        dsc = dsc_ref[...]
        b, c, u = b_ref[...], c_ref[...], u_ref[...]
        dcv = dsc * b
        dcv_n = jnp.where(last, 0.0, dscn_ref[...] * bn_ref[...])
        p_h = jnp.where(first, 0.0, ch_ref[...] * uh_ref[...])
        pcat = jnp.concatenate([p_h, c * u], axis=0)
        dccat = jnp.concatenate([dcv, dcv_n], axis=0)
        dp = jnp.zeros((tm, D_CONF), F32)
        for k in range(SC_K):
            j = SC_K - 1 - k
            dwsc_ref[k:k + 1, :] += _colsum(dcv * _shift_down(pcat, j))
            dp = dp + wsc_ref[k:k + 1, :] * _shift_up(dccat, j, tm)
        dr_ref[:, 2 * D_CONF:3 * D_CONF] = (dsc * cv_ref[...]).astype(BF16)
        dr_ref[:, 3 * D_CONF:4 * D_CONF] = (dp * u).astype(BF16)
        dr_ref[:, 4 * D_CONF:5 * D_CONF] = (dp * c).astype(BF16)

        tri = (lax.broadcasted_iota(jnp.int32, (tm, tm), 0) <= lax.broadcasted_iota(jnp.int32, (tm, tm), 1))
        dlogf = jnp.dot(tri.astype(F32), dcc_ref[...], precision=HIGHEST, preferred_element_type=F32) + carry[0:1, :]
        carry[0:1, :] = dlogf[0:1, :]
        dz = dlogf * _sigmoid(-(f_ref[...] + vec_ref[4:5, :N_F_PAD]))
        dvec_ref[4:5, 0:N_F_PAD] += _colsum(dz)
        dr_ref[:, 5 * D_CONF:] = dz.astype(BF16)

    rt = lambda i: nt - 1 - i
    cur = lambda j: pl.BlockSpec((tm, D_CONF), lambda i: (rt(i), j))
    prev = lambda j: pl.BlockSpec((HALO, D_CONF), lambda i: (jnp.maximum(rt(i) * per - 1, 0), j))
    nxt = lambda j: pl.BlockSpec((HALO, D_CONF), lambda i: (jnp.minimum((rt(i) + 1) * per, n_halo_blocks - 1), j))
    full = lambda x: pl.BlockSpec(x.shape, lambda i: (0, 0))
    acc = lambda r, w: pl.BlockSpec((r, w), lambda i: (0, 0))
    return pl.pallas_call(
        body, name=name, grid=(nt,),
        in_specs=[cur(0), cur(1), nxt(0), nxt(1), cur(0), nxt(0), cur(0),
                  cur(0), cur(1), prev(0), prev(1), cur(2), nxt(2), cur(3), cur(4), prev(3), prev(4),
                  pl.BlockSpec((tm, N_F_PAD), lambda i: (rt(i), F_BLK)),
                  pl.BlockSpec((tm, N_F_PAD), lambda i: (rt(i), 0)),
                  full(wdw), full(vec), full(wpw), full(wsc)],
        out_specs=[pl.BlockSpec((tm, N_REST), lambda i: (rt(i), 0)), acc(32, D_CONF), acc(8, D_CONF),
                   acc(D_CONF, D_CONF), acc(8, D_CONF)],
        out_shape=[SDS((lp, N_REST), BF16), SDS((32, D_CONF), F32), SDS((8, D_CONF), F32),
                   SDS((D_CONF, D_CONF), F32), SDS((8, D_CONF), F32)],
        scratch_shapes=[pltpu.VMEM((8, N_F_PAD), F32)],
        compiler_params=_cparams(("arbitrary",), 48),
    )(dcs, dcs, dcs, dcs, dwc, dwc, cv, rest, rest, rest, rest, rest, rest, rest, rest, rest, rest, rest, dccol,
      wdw, vec, wpw, wsc)


def _causal_tiles(n, key_major):
    if key_major:
        pairs = [(q, k) for k in range(n) for q in range(k, n)]
    else:
        pairs = [(q, k) for q in range(n) for k in range(q + 1)]
    return (jnp.asarray([p[0] for p in pairs], jnp.int32), jnp.asarray([p[1] for p in pairs], jnp.int32))


def _attn_fwd(qkv, crow, cpair, name):
    lp = qkv.shape[0]
    t = _row_tile(lp)
    nq = lp // t

    qtab, ktab = _causal_tiles(nq, key_major=False)

    def body(qt_ref, kt_ref, q_ref, k_ref, v_ref, c_ref, cq_ref, o_ref, lse_ref,
             m_scr, l_scr, al_scr, acc_scr, s_scr, p_scr):
        qi, ki = qt_ref[pl.program_id(1)], kt_ref[pl.program_id(1)]

        @pl.when(ki == 0)
        def _():
            lane = lax.broadcasted_iota(jnp.int32, (t, PAIR_W), 1)
            m_scr[...] = jnp.full_like(m_scr, NEG)
            l_scr[...] = jnp.where(lane < 2, 0.0, 1.0)
            acc_scr[...] = jnp.zeros_like(acc_scr)

        def step(diagonal):
            q, k, v = q_ref[...], k_ref[...], v_ref[...]
            lo = lax.broadcasted_iota(jnp.int32, (t, PAIR_W), 1) < HEAD_DIM
            pv = []
            for j in range(2):
                sel = lo if j == 0 else jnp.logical_not(lo)
                s_scr[...] = _dot_nt(jnp.where(sel, q, jnp.zeros_like(q)), k)

                def chunk(r, carry):
                    rows = pl.ds(pl.multiple_of(r * ATTN_CHUNK, ATTN_CHUNK), ATTN_CHUNK)
                    s = s_scr[rows, :] + (cq_ref[rows, j:j + 1] - c_ref[j:j + 1, :])
                    if diagonal:
                        row_id = r * ATTN_CHUNK + lax.broadcasted_iota(jnp.int32, (ATTN_CHUNK, t), 0)
                        s = jnp.where(row_id >= lax.broadcasted_iota(jnp.int32, (ATTN_CHUNK, t), 1), s, -jnp.inf)
                    m_prev = m_scr[rows, j:j + 1]
                    m_new = jnp.maximum(m_prev, jnp.max(s, axis=1, keepdims=True))
                    al = jnp.exp(m_prev - m_new)
                    p = jnp.exp(s - m_new)
                    l_scr[rows, j:j + 1] = al * l_scr[rows, j:j + 1] + jnp.sum(p, axis=1, keepdims=True)
                    m_scr[rows, j:j + 1] = m_new
                    al_scr[rows, j:j + 1] = al
                    p_scr[rows, :] = p.astype(BF16)
                    return carry

                lax.fori_loop(0, t // ATTN_CHUNK, chunk, 0)
                pv.append(_dot(p_scr[...], v))
            acc_scr[...] = (acc_scr[...] * jnp.where(lo, al_scr[:, 0:1], al_scr[:, 1:2])
                            + jnp.where(lo, pv[0], pv[1]))

        @pl.when(ki < qi)
        def _():
            step(False)

        @pl.when(ki == qi)
        def _():
            step(True)
            lo = lax.broadcasted_iota(jnp.int32, (t, PAIR_W), 1) < HEAD_DIM
            inv = jnp.where(lo, 1.0 / l_scr[:, 0:1], 1.0 / l_scr[:, 1:2])
            o_ref[...] = (acc_scr[...] * inv).astype(BF16)
            lse_ref[...] = m_scr[...] + jnp.log(l_scr[...])

    kv = lambda off: pl.BlockSpec((t, PAIR_W), lambda p, s, qt, kt: (kt[s], off + p))
    qblk = pl.BlockSpec((t, PAIR_W), lambda p, s, qt, kt: (qt[s], p))
    return pl.pallas_call(
        body, name=name,
        grid_spec=pltpu.PrefetchScalarGridSpec(
            num_scalar_prefetch=2, grid=(N_PAIRS, qtab.shape[0]),
            in_specs=[qblk, kv(N_PAIRS), kv(2 * N_PAIRS),
                      pl.BlockSpec((None, 2, t), lambda p, s, qt, kt: (p, 0, kt[s])), qblk],
            out_specs=[qblk, qblk],
            scratch_shapes=[pltpu.VMEM((t, PAIR_W), F32), pltpu.VMEM((t, PAIR_W), F32), pltpu.VMEM((t, PAIR_W), F32),
                            pltpu.VMEM((t, PAIR_W), F32), pltpu.VMEM((t, t), F32), pltpu.VMEM((t, t), BF16)]),
        out_shape=[SDS((lp, D_ATTN), BF16), SDS((lp, D_ATTN), F32)],
        compiler_params=_cparams(("parallel", "arbitrary"), 48),
    )(qtab, ktab, qkv, qkv, qkv, crow, cpair)


def _attn_bwd(qkv, do, crow, cpair, lse, delta, name):
    lp = qkv.shape[0]
    t = _row_tile(lp)
    nq = lp // t
    qtab, ktab = _causal_tiles(nq, key_major=True)

    def body(qt_ref, kt_ref, q_ref, k_ref, v_ref, do_ref, c_ref, cq_ref, lse_ref, dl_ref,
             dq_ref, dk_ref, dv_ref, dc_ref, dcq_ref, dq_scr, dcq_scr, dk_scr, dv_scr, dc_scr,
             s_scr, dp_scr, pb_scr, dsb_scr):
        step_id = pl.program_id(1)
        qi, ki = qt_ref[step_id], kt_ref[step_id]

        @pl.when(step_id == 0)
        def _():
            dq_scr[...] = jnp.zeros_like(dq_scr)
            dcq_scr[...] = jnp.zeros_like(dcq_scr)

        @pl.when(qi == ki)
        def _():
            dk_scr[...] = jnp.zeros_like(dk_scr)
            dv_scr[...] = jnp.zeros_like(dv_scr)
            dc_scr[...] = jnp.zeros_like(dc_scr)

        def step(diagonal):
            q, k, v, dout = q_ref[...], k_ref[...], v_ref[...], do_ref[...]
            lo = lax.broadcasted_iota(jnp.int32, (t, PAIR_W), 1) < HEAD_DIM
            zero = jnp.zeros_like(q)
            dq_new = jnp.zeros((t, PAIR_W), F32)
            dv_new = jnp.zeros((t, PAIR_W), F32)
            dk_new = jnp.zeros((t, PAIR_W), F32)
            col_sums = []
            q_base = qi * t
            for j in range(2):
                sel = lo if j == 0 else jnp.logical_not(lo)
                qj, kj, doj = jnp.where(sel, q, zero), jnp.where(sel, k, zero), jnp.where(sel, dout, zero)
                s_scr[...] = _dot_nt(qj, k)
                dp_scr[...] = _dot_nt(doj, v)

                def chunk(r, col_acc):
                    rows = pl.ds(pl.multiple_of(r * ATTN_CHUNK, ATTN_CHUNK), ATTN_CHUNK)
                    p = jnp.exp(s_scr[rows, :] + (cq_ref[rows, j:j + 1] - c_ref[j:j + 1, :]) - lse_ref[rows, j:j + 1])
                    if diagonal:
                        row_id = r * ATTN_CHUNK + lax.broadcasted_iota(jnp.int32, (ATTN_CHUNK, t), 0)
                        p = jnp.where(row_id >= lax.broadcasted_iota(jnp.int32, (ATTN_CHUNK, t), 1), p, 0.0)
                    ds = p * (dp_scr[rows, :] - dl_ref[rows, j:j + 1])
                    pb_scr[rows, :] = p.astype(BF16)
                    dsb_scr[rows, :] = ds.astype(BF16)
                    q_rows = pl.ds(pl.multiple_of(q_base + r * ATTN_CHUNK, ATTN_CHUNK), ATTN_CHUNK)
                    dcq_scr[q_rows, j:j + 1] += jnp.sum(ds, axis=1, keepdims=True)
                    for g in range(ATTN_CHUNK // 8):
                        col_acc = col_acc + ds[8 * g:8 * (g + 1), :]
                    return col_acc

                col_acc = lax.fori_loop(0, t // ATTN_CHUNK, chunk, jnp.zeros((8, t), F32))
                col_sums.append(_colsum(col_acc))
                dsb = dsb_scr[...]
                dv_new = dv_new + _dot_tn(pb_scr[...], doj)
                dk_new = dk_new + _dot_tn(dsb, qj)
                dq_new = dq_new + _dot(dsb, kj)
            dv_scr[...] += dv_new
            dk_scr[...] += dk_new
            dc_scr[0:2, :] -= jnp.concatenate(col_sums, axis=0)
            rows = pl.ds(pl.multiple_of(q_base, t), t)
            dq_scr[rows, :] += dq_new

        @pl.when(qi > ki)
        def _():
            step(False)

        @pl.when(qi == ki)
        def _():
            step(True)
            rows = pl.ds(pl.multiple_of(qi * t, t), t)
            dq_ref[...] = (dq_scr[rows, :] * (HEAD_DIM ** -0.5)).astype(BF16)
            dcq_ref[...] = dcq_scr[rows, :]

        @pl.when(qi == nq - 1)
        def _():
            dk_ref[...] = dk_scr[...].astype(BF16)
            dv_ref[...] = dv_scr[...].astype(BF16)
            dc_ref[...] = dc_scr[0:2, :]

    qside = lambda off: pl.BlockSpec((t, PAIR_W), lambda p, s, qt, kt: (qt[s], off + p))
    kside = lambda off: pl.BlockSpec((t, PAIR_W), lambda p, s, qt, kt: (kt[s], off + p))
    crow_blk = pl.BlockSpec((None, 2, t), lambda p, s, qt, kt: (p, 0, kt[s]))
    return pl.pallas_call(
        body, name=name,
        grid_spec=pltpu.PrefetchScalarGridSpec(
            num_scalar_prefetch=2, grid=(N_PAIRS, qtab.shape[0]),
            in_specs=[qside(0), kside(N_PAIRS), kside(2 * N_PAIRS), qside(0), crow_blk, qside(0), qside(0), qside(0)],
            out_specs=[kside(0), kside(0), kside(0), crow_blk, kside(0)],
            scratch_shapes=[pltpu.VMEM((lp, PAIR_W), F32), pltpu.VMEM((lp, PAIR_W), F32), pltpu.VMEM((t, PAIR_W), F32),
                            pltpu.VMEM((t, PAIR_W), F32), pltpu.VMEM((8, t), F32),
                            pltpu.VMEM((t, t), F32), pltpu.VMEM((t, t), F32), pltpu.VMEM((t, t), BF16),
                            pltpu.VMEM((t, t), BF16)]),
        out_shape=[SDS((lp, D_ATTN), BF16), SDS((lp, D_ATTN), BF16), SDS((lp, D_ATTN), BF16),
                   SDS((N_PAIRS, 2, lp), F32), SDS((lp, D_ATTN), F32)],
        compiler_params=_cparams(("parallel", "arbitrary"), 48),
    )(qtab, ktab, qkv, qkv, qkv, do, crow, cpair, lse, delta)


def _shard_sum(own, parts, name, slot_order=False):
    rows, cols = own.shape
    tr = rows if rows <= 512 else 512
    assert rows % tr == 0

    def body(me_ref, own_ref, parts_ref, o_ref):
        if slot_order:
            acc = parts_ref[0].astype(F32)
            for j in range(1, 4):
                acc = acc + parts_ref[j].astype(F32)
        else:
            me = me_ref[0]
            acc = own_ref[...]
            for j in range(4):
                acc = acc + jnp.where(me == j, 0.0, parts_ref[j].astype(F32))
        o_ref[...] = acc

    me = (2 * lax.axis_index("x") + lax.axis_index("y")).astype(jnp.int32).reshape(1)
    return pl.pallas_call(
        body, name=name,
        grid_spec=pltpu.PrefetchScalarGridSpec(
            num_scalar_prefetch=1, grid=(rows // tr,),
            in_specs=[pl.BlockSpec((tr, cols), lambda i, me: (i, 0)), pl.BlockSpec((4, tr, cols), lambda i, me: (0, i, 0))],
            out_specs=pl.BlockSpec((tr, cols), lambda i, me: (i, 0))),
        out_shape=SDS((rows, cols), F32),
        compiler_params=_cparams(("parallel",), 48),
    )(me, own, parts)


def _adamw(w, m, v, s_own, s_sib, name):
    rows, cols = w.shape
    tr = rows if rows <= 256 else 256
    assert rows % tr == 0

    def body(w_ref, m_ref, v_ref, a_ref, b_ref, g_ref, d_ref, nm_ref, nv_ref):
        g = a_ref[...] + b_ref[...]
        g_ref[...] = g
        m = ADAM_B1 * m_ref[...] + (1.0 - ADAM_B1) * g
        v = ADAM_B2 * v_ref[...] + (1.0 - ADAM_B2) * jnp.square(g)
        nm_ref[...] = m
        nv_ref[...] = v
        m_hat = m / (1.0 - ADAM_B1 ** ADAM_STEP)
        v_hat = v / (1.0 - ADAM_B2 ** ADAM_STEP)
        d_ref[...] = -ADAM_LR * (m_hat / (jnp.sqrt(v_hat) + ADAM_EPS) + ADAM_WD * w_ref[...])

    blk = pl.BlockSpec((tr, cols), lambda i: (i, 0))
    return pl.pallas_call(
        body, name=name, grid=(rows // tr,), in_specs=[blk] * 5, out_specs=[blk] * 4,
        out_shape=[SDS((rows, cols), F32)] * 4,
        compiler_params=_cparams(("parallel",), 48),
    )(w, m, v, s_own, s_sib)


def _pad_rows(a, rows):
    return jnp.pad(a, ((0, rows - a.shape[0]), (0, 0)))


def _flat_rows(arrs, cols, rows):
    flat = jnp.concatenate([a.reshape(-1) for a in arrs])
    return jnp.pad(flat, (0, rows * cols - flat.shape[0])).reshape(rows, cols)


def _unflat(buf, shapes):
    flat, out, off = buf.reshape(-1), [], 0
    for s in shapes:
        n = 1
        for d in s:
            n *= d
        out.append(flat[off:off + n].reshape(s))
        off += n
    return out


def kernel(x, meta_tokens, mix_norm_g, w_in, b_forget, w_conf_dw, b_conf_dw, conf_ln_g, conf_ln_b, w_conf_pw, b_conf_pw, w_sc_conv, w_out, mlp_norm_g, w_mlp1, w_mlp2, final_norm_g, loss_target, m_meta_tokens, m_mix_norm_g, m_w_in, m_b_forget, m_w_conf_dw, m_b_conf_dw, m_conf_ln_g, m_conf_ln_b, m_w_conf_pw, m_b_conf_pw, m_w_sc_conv, m_w_out, m_mlp_norm_g, m_w_mlp1, m_w_mlp2, m_final_norm_g, v_meta_tokens, v_mix_norm_g, v_w_in, v_b_forget, v_w_conf_dw, v_b_conf_dw, v_conf_ln_g, v_conf_ln_b, v_w_conf_pw, v_b_conf_pw, v_w_sc_conv, v_w_out, v_mlp_norm_g, v_w_mlp1, v_w_mlp2, v_final_norm_g):
    depth = w_in.shape[0]
    seq = x.shape[1]
    l_tok = N_META + seq
    lp = _padded_len(l_tok)
    n_in = w_in.shape[2] * 4
    n_in_sh = w_in.shape[2]
    me = 2 * lax.axis_index("x") + lax.axis_index("y")

    small_shapes = [w_conf_dw.shape, w_conf_pw.shape, w_sc_conv.shape, meta_tokens.shape]
    n_small = sum(a.size for a in (w_conf_dw, w_conf_pw, w_sc_conv, meta_tokens))
    small_rows = -(-n_small // 1024 // 8) * 8
    small = _flat_rows([w_conf_dw, w_conf_pw, w_sc_conv, meta_tokens], 1024, small_rows)
    g_in, g_out, g_m1, g_m2, g_small = _chip_exchange(
        [w_in.astype(BF16).reshape(depth * D_MODEL, n_in_sh), w_out.astype(BF16).reshape(-1, D_MODEL),
         w_mlp1.astype(BF16).reshape(depth * D_MODEL, -1), w_mlp2.astype(BF16).reshape(-1, D_MODEL), small],
        gather=True, name="weights_all_gather")
    w_in_f = jnp.concatenate([g_in[j].reshape(depth, D_MODEL, n_in_sh) for j in range(4)], axis=2)
    w_out_f = jnp.concatenate([g_out[j].reshape(depth, -1, D_MODEL) for j in range(4)], axis=1)
    w_m1_f = jnp.concatenate([g_m1[j].reshape(depth, D_MODEL, -1) for j in range(4)], axis=2)
    w_m2_f = jnp.concatenate([g_m2[j].reshape(depth, -1, D_MODEL) for j in range(4)], axis=1)
    sm = [_unflat(g_small[j], small_shapes) for j in range(4)]
    w_dw_f = jnp.concatenate([s[0] for s in sm], axis=2)
    w_pw_f = jnp.concatenate([s[1] for s in sm], axis=1)
    w_sc_f = jnp.concatenate([s[2] for s in sm], axis=2)
    meta_f = jnp.concatenate([s[3] for s in sm], axis=1)

    q_end, f_end = 3 * D_ATTN, 3 * D_ATTN + N_HEADS
    wqkv = [w_in_f[l, :, :q_end] for l in range(depth)]
    wrest = [jnp.concatenate([w_in_f[l, :, f_end:], w_in_f[l, :, q_end:f_end],
                              jnp.zeros((D_MODEL, N_F_PAD - N_HEADS), BF16)], axis=1) for l in range(depth)]
    wdw = [_pad_rows(w_dw_f[l], 32) for l in range(depth)]
    wsc = [_pad_rows(w_sc_f[l], 8) for l in range(depth)]
    wpw = [w_pw_f[l].astype(BF16) for l in range(depth)]
    vec = [_pad_rows(jnp.stack([b_conf_dw[l], conf_ln_g[l], conf_ln_b[l], b_conf_pw[l],
                                jnp.pad(b_forget[l], (0, D_CONF - N_HEADS))]), 8) for l in range(depth)]
    col = jnp.arange(D_ATTN)
    seg = (col[None, :] == ((col // PAIR_W) * PAIR_W + (col % PAIR_W) // HEAD_DIM)[:, None]).astype(F32)

    h = jnp.concatenate([meta_f, x[0], jnp.zeros((lp - l_tok, D_MODEL), F32)], axis=0)
    target = jnp.pad(loss_target[0], ((N_META, lp - l_tok), (0, 0)))
    saved = []
    for l in range(depth):
        g1, g2 = mix_norm_g[l][None, :], mlp_norm_g[l][None, :]
        hn1, qkv, rest = _proj_fwd(h, g1, wqkv[l], wrest[l], name=f"proj_fwd_{l}")
        cs, dwc, cv, ccol = _conv_fwd(rest, wdw[l], vec[l], wpw[l], wsc[l], name=f"conv_fwd_{l}")
        crow = ccol[:, :N_HEADS].T.reshape(N_PAIRS, 2, lp)
        cpair = jnp.pad(ccol[:, :N_HEADS].reshape(lp, N_PAIRS, 2), ((0, 0), (0, 0), (0, PAIR_W - 2))).reshape(lp, D_ATTN)
        attn, lse = _attn_fwd(qkv, crow, cpair, name=f"attn_fwd_{l}")
        h1 = _mix_out(h, attn, cs, w_out_f[l, :D_ATTN], w_out_f[l, D_ATTN:], name=f"mix_out_{l}")
        hn2, a, r, h2 = _mlp_fwd(h1, g2, w_m1_f[l], w_m2_f[l], name=f"mlp_fwd_{l}")
        saved.append((h, g1, g2, hn1, qkv, rest, cs, dwc, cv, crow, cpair, attn, lse, h1, hn2, a, r))
        h = h2
    dh, loss11, d_final_g = _loss_head(h, target, final_norm_g[None, :], seq, name="loss_head")
    loss = lax.psum(loss11[0, 0], ("x", "y", "c"))

    grads = [None] * depth
    for l in reversed(range(depth)):
        h0, g1, g2, hn1, qkv, rest, cs, dwc, cv, crow, cpair, attn, lse, h1, hn2, a, r = saved[l]
        da, dh1, dg2 = _mlp_bwd(dh, h1, g2, a, w_m1_f[l], w_m2_f[l], name=f"mlp_bwd_{l}")
        d_w2 = _mm_tn(r, dh, name=f"dw_mlp2_{l}")
        d_w1 = _mm_tn(hn2, da, name=f"dw_mlp1_{l}")
        do, dcs, delta = _mix_bwd(dh1, w_out_f[l, :D_ATTN], w_out_f[l, D_ATTN:], attn, seg, name=f"mix_bwd_{l}")
        d_wout = jnp.concatenate([_mm_tn(attn, dh1, name=f"dw_out_a_{l}"), _mm_tn(cs, dh1, name=f"dw_out_c_{l}")], axis=0)
        dq, dk, dv, dcrow, dcq = _attn_bwd(qkv, do, crow, cpair, lse, delta, name=f"attn_bwd_{l}")
        dc_heads = dcrow.reshape(N_HEADS, lp).T + dcq.reshape(lp, N_PAIRS, PAIR_W)[:, :, :2].reshape(lp, N_HEADS)
        dccol = jnp.pad(dc_heads, ((0, 0), (0, N_F_PAD - N_HEADS)))
        drest, d_wdw, d_wsc, d_wpw, d_vec = _conv_bwd(dcs, dwc, cv, rest, dccol, wdw[l], vec[l], wpw[l], wsc[l],
                                                      name=f"conv_bwd_{l}")
        dh, dg1 = _in_bwd(dh1, h0, g1, dq, dk, dv, drest, wqkv[l], wrest[l], name=f"in_bwd_{l}")
        d_wq = _mm_tn(hn1, dq, name=f"dw_q_{l}")
        d_wk = _mm_tn(hn1, dk, name=f"dw_k_{l}")
        d_wv = _mm_tn(hn1, dv, name=f"dw_v_{l}")
        d_wr = _mm_tn(hn1, drest, name=f"dw_rest_{l}")
        d_win = jnp.concatenate([d_wq, d_wk, d_wv, d_wr[:, 5 * D_CONF:5 * D_CONF + N_HEADS], d_wr[:, :5 * D_CONF]], axis=1)
        grads[l] = dict(w_in=d_win, w_out=d_wout, w_mlp1=d_w1, w_mlp2=d_w2, mix_norm_g=dg1[0], mlp_norm_g=dg2[0],
                        w_conf_dw=d_wdw[:CONF_K], w_sc_conv=d_wsc[:SC_K], w_conf_pw=d_wpw, b_conf_dw=d_vec[0],
                        conf_ln_g=d_vec[1], conf_ln_b=d_vec[2], b_conf_pw=d_vec[3], b_forget=d_vec[4, :N_HEADS])
    grad_x = dh[N_META:N_META + seq][None]
    d_meta = dh[:N_META]

    def stack(k):
        return jnp.stack([grads[l][k] for l in range(depth)])

    def col_shards(a):
        d, rr, cc = a.shape
        return a.reshape(d, rr, 4, cc // 4).transpose(2, 0, 1, 3).reshape(4, d * rr, cc // 4)

    def row_shards(a):
        d, rr, cc = a.shape
        return a.reshape(d, 4, rr // 4, cc).transpose(1, 0, 2, 3).reshape(4, d * rr // 4, cc)

    big = [col_shards(stack("w_in")), row_shards(stack("w_out")), col_shards(stack("w_mlp1")), row_shards(stack("w_mlp2"))]
    rep = [stack("mix_norm_g"), stack("b_forget"), stack("b_conf_dw"), stack("conf_ln_g"), stack("conf_ln_b"),
           stack("b_conf_pw"), stack("mlp_norm_g"), d_final_g[0]]
    rep_shapes = [a.shape for a in rep]
    sh_dw, sh_pw, sh_sc = col_shards(stack("w_conf_dw")), row_shards(stack("w_conf_pw")), col_shards(stack("w_sc_conv"))
    sh_meta = d_meta.reshape(N_META, 4, D_MODEL // 4).transpose(1, 0, 2)
    n_rep = sum(a.size for a in rep)
    gs_rows = -(-(n_small + n_rep) // 1024 // 8) * 8
    gsmall = jnp.stack([_flat_rows([sh_dw[j], sh_pw[j], sh_sc[j], sh_meta[j]] + rep, 1024, gs_rows) for j in range(4)])
    recv = _chip_exchange([b.astype(BF16) for b in big] + [gsmall], gather=False, name="grads_exchange")
    own = [lax.dynamic_index_in_dim(b, me, 0, keepdims=False) for b in big + [gsmall]]
    sums = [_shard_sum(own[i], recv[i], name=f"shard_sum_{i}", slot_order=(i == 4)) for i in range(5)]
    sibs = _sibling_exchange(sums, name="sibling_exchange")

    w_small = _flat_rows([w_conf_dw, w_conf_pw, w_sc_conv, meta_tokens, mix_norm_g, b_forget, b_conf_dw, conf_ln_g,
                          conf_ln_b, b_conf_pw, mlp_norm_g, final_norm_g], 1024, gs_rows)
    m_small = _flat_rows([m_w_conf_dw, m_w_conf_pw, m_w_sc_conv, m_meta_tokens, m_mix_norm_g, m_b_forget, m_b_conf_dw,
                          m_conf_ln_g, m_conf_ln_b, m_b_conf_pw, m_mlp_norm_g, m_final_norm_g], 1024, gs_rows)
    v_small = _flat_rows([v_w_conf_dw, v_w_conf_pw, v_w_sc_conv, v_meta_tokens, v_mix_norm_g, v_b_forget, v_b_conf_dw,
                          v_conf_ln_g, v_conf_ln_b, v_b_conf_pw, v_mlp_norm_g, v_final_norm_g], 1024, gs_rows)
    triples = [(w_in, m_w_in, v_w_in), (w_out, m_w_out, v_w_out), (w_mlp1, m_w_mlp1, v_w_mlp1),
               (w_mlp2, m_w_mlp2, v_w_mlp2), (w_small, m_small, v_small)]
    res = []
    for i, (w, m, v) in enumerate(triples):
        shp = w.shape
        two_d = sums[i].shape
        out4 = _adamw(w.reshape(two_d), m.reshape(two_d), v.reshape(two_d), sums[i], sibs[i], name=f"adamw_{i}")
        res.append([o.reshape(shp) for o in out4])
    small_out = [_unflat(res[4][k], small_shapes + rep_shapes) for k in range(4)]

    names = ["meta_tokens", "mix_norm_g", "w_in", "b_forget", "w_conf_dw", "b_conf_dw", "conf_ln_g", "conf_ln_b",
             "w_conf_pw", "b_conf_pw", "w_sc_conv", "w_out", "mlp_norm_g", "w_mlp1", "w_mlp2", "final_norm_g"]
    small_order = ["w_conf_dw", "w_conf_pw", "w_sc_conv", "meta_tokens", "mix_norm_g", "b_forget", "b_conf_dw",
                   "conf_ln_g", "conf_ln_b", "b_conf_pw", "mlp_norm_g", "final_norm_g"]
    big_order = {"w_in": 0, "w_out": 1, "w_mlp1": 2, "w_mlp2": 3}
    outs = []
    for k in range(4):
        for nme in names:
            outs.append(res[big_order[nme]][k] if nme in big_order else small_out[k][small_order.index(nme)])
    return (loss, grad_x, *outs)
```

```python
import jax
import jax.numpy as jnp
from jax import lax
from jax.experimental import pallas as pl
from jax.experimental.pallas import tpu as pltpu

F32, BF16 = jnp.float32, jnp.bfloat16
D_MODEL = 1024
D_ATTN = 512
D_CONF = 256
N_HEADS = 8
HEAD_DIM = 64
N_PAIRS = N_HEADS // 2
PAIR_W = 2 * HEAD_DIM
N_META = 16
D_FF = 4096
CONF_K = 31
SC_K = 3
HALO = 32
N_F_PAD = 128
N_REST = 5 * D_CONF + N_F_PAD
F_BLK = 5 * D_CONF // N_F_PAD
EPS = 1e-6
NEG = -1e30
BIG_TILE = 640
SMALL_TILE = 128
MLP_TILE_DIV = 2
V7X_VMEM_BYTES = 64 * 1024 * 1024
ADAM_LR, ADAM_B1, ADAM_B2, ADAM_EPS, ADAM_WD, ADAM_STEP = 0.001, 0.9, 0.999, 1e-08, 0.01, 10
MESH = pl.DeviceIdType.MESH
HIGHEST = lax.Precision.HIGHEST
SDS = jax.ShapeDtypeStruct


def _cparams(sem=None, vmem_mb=None):
    kw = {}
    if sem is not None:
        kw["dimension_semantics"] = sem
    if vmem_mb is not None:
        assert vmem_mb * 1024 * 1024 <= V7X_VMEM_BYTES
        kw["vmem_limit_bytes"] = vmem_mb * 1024 * 1024
    return pltpu.CompilerParams(**kw)


def _dot(a, b):
    return jnp.dot(a, b, preferred_element_type=F32)


def _dot_nt(a, b):
    return lax.dot_general(a, b, (((1,), (1,)), ((), ())), preferred_element_type=F32)


def _dot_tn(a, b):
    return lax.dot_general(a, b, (((0,), (0,)), ((), ())), preferred_element_type=F32)


def _sigmoid(x):
    return 1.0 / (1.0 + jnp.exp(-x))


def _rms_fwd(x, g):
    r = lax.rsqrt(jnp.mean(x * x, axis=-1, keepdims=True) + EPS)
    xn = x * r
    return r, xn, xn * g


def _rms_bwd(r, xn, g, dy):
    u = dy * g
    return r * (u - xn * jnp.mean(u * xn, axis=-1, keepdims=True))


def _colsum(x):
    return jnp.sum(x, axis=0, keepdims=True)


def _row_tile(lp):
    return BIG_TILE if lp % BIG_TILE == 0 else SMALL_TILE


def _padded_len(l):
    t = BIG_TILE if l >= BIG_TILE else SMALL_TILE
    return -(-l // t) * t


def _weights_all_gather(arrs, name):
    n = len(arrs)
    halves = [a.shape[0] // 2 for a in arrs]
    assert all(a.shape[0] == 2 * h and h % 8 == 0 for a, h in zip(arrs, halves))

    def body(*refs):
        ins, outs = refs[:n], refs[n:2 * n]
        ici_send, ici_recv, d2d_send, d2d_recv, local_sems = refs[2 * n:]
        x, y, c = lax.axis_index("x"), lax.axis_index("y"), lax.axis_index("c")
        me = 2 * x + y
        sibling = (x, y, 1 - c)
        chips = [(1 - x, y), (x, 1 - y), (1 - x, 1 - y)]

        def half(ref, i, which):
            return ref.at[pl.ds(pl.multiple_of(which * halves[i], 8), halves[i])]

        def copy(src, dst, send_sem, recv_sem, to):
            return pltpu.make_async_remote_copy(src_ref=src, dst_ref=dst, send_sem=send_sem, recv_sem=recv_sem,
                                                device_id=to, device_id_type=MESH)

        local, over_ici, landed, passed_on, from_sibling = [], [], [], [], []
        for i in range(n):
            local.append(pltpu.make_async_copy(ins[i], outs[i].at[me], local_sems.at[i]))
            for j, (px, py) in enumerate(chips):
                k = 3 * i + j
                peer = 2 * px + py
                mine, theirs = half(outs[i].at[me], i, c), half(outs[i].at[peer], i, c)
                over_ici.append(copy(half(ins[i], i, c), mine, ici_send.at[k], ici_recv.at[k], (px, py, c)))
                landed.append(copy(half(ins[i], i, c), theirs, ici_send.at[k], ici_recv.at[k], (px, py, c)))
                passed_on.append(copy(theirs, theirs, d2d_send.at[k], d2d_recv.at[k], sibling))
                other = half(outs[i].at[peer], i, 1 - c)
                from_sibling.append(copy(other, other, d2d_send.at[k], d2d_recv.at[k], sibling))
        for cp in local + over_ici:
            cp.start()
        for arrived, onward in zip(landed, passed_on):
            arrived.wait_recv()
            onward.start()
        for cp in from_sibling:
            cp.wait_recv()
        for cp in over_ici + passed_on:
            cp.wait_send()
        for cp in local:
            cp.wait()

    any_spec = pl.BlockSpec(memory_space=pl.ANY)
    return pl.pallas_call(
        body, name=name, out_shape=[SDS((4,) + a.shape, a.dtype) for a in arrs],
        in_specs=[any_spec] * n, out_specs=[any_spec] * n,
        scratch_shapes=[pltpu.SemaphoreType.DMA((3 * n,)) for _ in range(4)] + [pltpu.SemaphoreType.DMA((n,))],
    )(*arrs)


def _chip_exchange(arrs, name):
    n = len(arrs)

    def body(*refs):
        ins, outs = refs[:n], refs[n:2 * n]
        send_sems, recv_sems, local_sems = refs[2 * n:]
        x, y, c = lax.axis_index("x"), lax.axis_index("y"), lax.axis_index("c")
        me = 2 * x + y
        chips = [(1 - x, y), (x, 1 - y), (1 - x, 1 - y)]
        local, sends, recvs = [], [], []
        for i in range(n):
            local.append(pltpu.make_async_copy(ins[i].at[me], outs[i].at[me], local_sems.at[i]))
            for j, (px, py) in enumerate(chips):
                k = 3 * i + j
                peer = 2 * px + py
                src = ins[i].at[peer]
                sends.append(pltpu.make_async_remote_copy(
                    src_ref=src, dst_ref=outs[i].at[me], send_sem=send_sems.at[k], recv_sem=recv_sems.at[k],
                    device_id=(px, py, c), device_id_type=MESH))
                recvs.append(pltpu.make_async_remote_copy(
                    src_ref=src, dst_ref=outs[i].at[peer], send_sem=send_sems.at[k], recv_sem=recv_sems.at[k],
                    device_id=(px, py, c), device_id_type=MESH))
        for cp in local + sends:
            cp.start()
        for cp in recvs:
            cp.wait_recv()
        for cp in sends:
            cp.wait_send()
        for cp in local:
            cp.wait()

    any_spec = pl.BlockSpec(memory_space=pl.ANY)
    return pl.pallas_call(
        body, name=name, out_shape=[SDS(a.shape, a.dtype) for a in arrs],
        in_specs=[any_spec] * n, out_specs=[any_spec] * n,
        scratch_shapes=[pltpu.SemaphoreType.DMA((3 * n,)), pltpu.SemaphoreType.DMA((3 * n,)),
                        pltpu.SemaphoreType.DMA((n,))],
    )(*arrs)


def _sibling_exchange(arrs, name):
    n = len(arrs)

    def body(*refs):
        ins, outs = refs[:n], refs[n:2 * n]
        send_sems, recv_sems = refs[2 * n:]
        sib = (lax.axis_index("x"), lax.axis_index("y"), 1 - lax.axis_index("c"))
        cps = [pltpu.make_async_remote_copy(src_ref=ins[i], dst_ref=outs[i], send_sem=send_sems.at[i],
                                            recv_sem=recv_sems.at[i], device_id=sib, device_id_type=MESH)
               for i in range(n)]
        for cp in cps:
            cp.start()
        for cp in cps:
            cp.wait_recv()
        for cp in cps:
            cp.wait_send()

    any_spec = pl.BlockSpec(memory_space=pl.ANY)
    return pl.pallas_call(
        body, name=name, out_shape=[SDS(a.shape, a.dtype) for a in arrs],
        in_specs=[any_spec] * n, out_specs=[any_spec] * n,
        scratch_shapes=[pltpu.SemaphoreType.DMA((n,)), pltpu.SemaphoreType.DMA((n,))],
    )(*arrs)


def _proj_fwd(h, g, wqkv, wrest, name):
    lp = h.shape[0]
    tm = _row_tile(lp)

    def body(h_ref, g_ref, wq_ref, wr_ref, hn_ref, qkv_ref, rest_ref):
        _, _, y = _rms_fwd(h_ref[...], g_ref[...])
        hn = y.astype(BF16)
        hn_ref[...] = hn
        qkv = _dot(hn, wq_ref[...])
        qkv_ref[:, :D_ATTN] = (qkv[:, :D_ATTN] * (HEAD_DIM ** -0.5)).astype(BF16)
        qkv_ref[:, D_ATTN:] = qkv[:, D_ATTN:].astype(BF16)
        rest_ref[...] = _dot(hn, wr_ref[...])

    row = lambda w: pl.BlockSpec((tm, w), lambda i: (i, 0))
    full = lambda a: pl.BlockSpec(a.shape, lambda i: (0, 0))
    return pl.pallas_call(
        body, name=name, grid=(lp // tm,),
        in_specs=[row(D_MODEL), full(g), full(wqkv), full(wrest)],
        out_specs=[row(D_MODEL), row(3 * D_ATTN), row(N_REST)],
        out_shape=[SDS((lp, D_MODEL), BF16), SDS((lp, 3 * D_ATTN), BF16), SDS((lp, N_REST), F32)],
        compiler_params=_cparams(("parallel",), 48),
    )(h, g, wqkv, wrest)


def _mm_tn(a, b, name):
    kk, m = a.shape
    n = b.shape[1]
    tk = _row_tile(kk)
    tmo = min(m, 1024)
    tno = n if n <= 1536 else 1024

    def body(a_ref, b_ref, o_ref):
        @pl.when(pl.program_id(2) == 0)
        def _():
            o_ref[...] = jnp.zeros_like(o_ref)
        o_ref[...] += _dot_tn(a_ref[...], b_ref[...].astype(BF16))

    return pl.pallas_call(
        body, name=name, grid=(m // tmo, n // tno, kk // tk),
        in_specs=[pl.BlockSpec((tk, tmo), lambda i, j, k: (k, i)), pl.BlockSpec((tk, tno), lambda i, j, k: (k, j))],
        out_specs=pl.BlockSpec((tmo, tno), lambda i, j, k: (i, j)),
        out_shape=SDS((m, n), F32),
        compiler_params=_cparams(("parallel", "parallel", "arbitrary"), 48),
    )(a, b)


def _mix_out(h, attn, cs, wa, wc, name):
    lp = h.shape[0]
    tm = _row_tile(lp)

    def body(h_ref, at_ref, cs_ref, wa_ref, wc_ref, o_ref):
        o_ref[...] = h_ref[...] + _dot(at_ref[...], wa_ref[...]) + _dot(cs_ref[...], wc_ref[...])

    row = lambda w: pl.BlockSpec((tm, w), lambda i: (i, 0))
    full = lambda a: pl.BlockSpec(a.shape, lambda i: (0, 0))
    return pl.pallas_call(
        body, name=name, grid=(lp // tm,),
        in_specs=[row(D_MODEL), row(D_ATTN), row(D_ATTN), full(wa), full(wc)],
        out_specs=row(D_MODEL), out_shape=SDS((lp, D_MODEL), F32),
        compiler_params=_cparams(("parallel",), 48),
    )(h, attn, cs, wa, wc)


def _mlp_fwd(h, g, w1, w2, name):
    lp = h.shape[0]
    tm = _row_tile(lp) // MLP_TILE_DIV

    def body(h_ref, g_ref, w1_ref, w2_ref, hn_ref, a_ref, r_ref, o_ref):
        x = h_ref[...]
        _, _, y = _rms_fwd(x, g_ref[...])
        hn = y.astype(BF16)
        hn_ref[...] = hn
        a = _dot(hn, w1_ref[...])
        a_ref[...] = a
        r = jnp.square(jnp.maximum(a, 0.0)).astype(BF16)
        r_ref[...] = r
        o_ref[...] = x + _dot(r, w2_ref[...])

    row = lambda w: pl.BlockSpec((tm, w), lambda i: (i, 0))
    vmem = pl.BlockSpec(memory_space=pltpu.VMEM)
    return pl.pallas_call(
        body, name=name, grid=(lp // tm,),
        in_specs=[row(D_MODEL), pl.BlockSpec(g.shape, lambda i: (0, 0)), vmem, vmem],
        out_specs=[row(D_MODEL), row(D_FF), row(D_FF), row(D_MODEL)],
        out_shape=[SDS((lp, D_MODEL), BF16), SDS((lp, D_FF), F32), SDS((lp, D_FF), BF16), SDS((lp, D_MODEL), F32)],
        compiler_params=_cparams(("parallel",), 60),
    )(h, g, w1, w2)


def _mlp_bwd(dh2, h1, g, a, w1, w2, name):
    lp = h1.shape[0]
    tm = _row_tile(lp) // MLP_TILE_DIV

    def body(dy_ref, h_ref, g_ref, a_ref, w1_ref, w2_ref, da_ref, dh_ref, dg_ref):
        @pl.when(pl.program_id(0) == 0)
        def _():
            dg_ref[...] = jnp.zeros_like(dg_ref)
        dy = dy_ref[...]
        dr = _dot_nt(dy.astype(BF16), w2_ref[...])
        da = (dr * (2.0 * jnp.maximum(a_ref[...], 0.0))).astype(BF16)
        da_ref[...] = da
        dhn = _dot_nt(da, w1_ref[...])
        r, xn, _ = _rms_fwd(h_ref[...], g_ref[...])
        dg_ref[...] += _colsum(dhn * xn)
        dh_ref[...] = dy + _rms_bwd(r, xn, g_ref[...], dhn)

    row = lambda w: pl.BlockSpec((tm, w), lambda i: (i, 0))
    vmem = pl.BlockSpec(memory_space=pltpu.VMEM)
    vec = pl.BlockSpec((1, D_MODEL), lambda i: (0, 0))
    return pl.pallas_call(
        body, name=name, grid=(lp // tm,),
        in_specs=[row(D_MODEL), row(D_MODEL), vec, row(D_FF), vmem, vmem],
        out_specs=[row(D_FF), row(D_MODEL), vec],
        out_shape=[SDS((lp, D_FF), BF16), SDS((lp, D_MODEL), F32), SDS((1, D_MODEL), F32)],
        compiler_params=_cparams(("arbitrary",), 60),
    )(dh2, h1, g, a, w1, w2)


def _mix_bwd(dh1, wa, wc, attn, seg, name):
    lp = dh1.shape[0]
    tm = _row_tile(lp)

    def body(dy_ref, wa_ref, wc_ref, o_ref, seg_ref, do_ref, dcs_ref, dl_ref):
        dy = dy_ref[...].astype(BF16)
        do = _dot_nt(dy, wa_ref[...])
        do_ref[...] = do.astype(BF16)
        dcs_ref[...] = _dot_nt(dy, wc_ref[...])
        dl_ref[...] = jnp.dot(do * o_ref[...].astype(F32), seg_ref[...], precision=HIGHEST,
                              preferred_element_type=F32)

    row = lambda w: pl.BlockSpec((tm, w), lambda i: (i, 0))
    full = lambda x: pl.BlockSpec(x.shape, lambda i: (0, 0))
    return pl.pallas_call(
        body, name=name, grid=(lp // tm,),
        in_specs=[row(D_MODEL), full(wa), full(wc), row(D_ATTN), full(seg)],
        out_specs=[row(D_ATTN), row(D_ATTN), row(D_ATTN)],
        out_shape=[SDS((lp, D_ATTN), BF16), SDS((lp, D_ATTN), F32), SDS((lp, D_ATTN), F32)],
        compiler_params=_cparams(("parallel",), 48),
    )(dh1, wa, wc, attn, seg)


def _in_bwd(dh1, h, g, dq, dk, dv, drest, wqkv, wrest, name):
    lp = h.shape[0]
    tm = _row_tile(lp)

    def body(dy_ref, h_ref, g_ref, dq_ref, dk_ref, dv_ref, dr_ref, wq_ref, wr_ref, dh_ref, dg_ref):
        @pl.when(pl.program_id(0) == 0)
        def _():
            dg_ref[...] = jnp.zeros_like(dg_ref)
        dhn = (_dot_nt(dq_ref[...], wq_ref[:, 0:D_ATTN]) + _dot_nt(dk_ref[...], wq_ref[:, D_ATTN:2 * D_ATTN])
               + _dot_nt(dv_ref[...], wq_ref[:, 2 * D_ATTN:]) + _dot_nt(dr_ref[...], wr_ref[...]))
        r, xn, _ = _rms_fwd(h_ref[...], g_ref[...])
        dg_ref[...] += _colsum(dhn * xn)
        dh_ref[...] = dy_ref[...] + _rms_bwd(r, xn, g_ref[...], dhn)

    row = lambda w: pl.BlockSpec((tm, w), lambda i: (i, 0))
    full = lambda x: pl.BlockSpec(x.shape, lambda i: (0, 0))
    vec = pl.BlockSpec((1, D_MODEL), lambda i: (0, 0))
    return pl.pallas_call(
        body, name=name, grid=(lp // tm,),
        in_specs=[row(D_MODEL), row(D_MODEL), vec, row(D_ATTN), row(D_ATTN), row(D_ATTN), row(N_REST),
                  full(wqkv), full(wrest)],
        out_specs=[row(D_MODEL), vec],
        out_shape=[SDS((lp, D_MODEL), F32), SDS((1, D_MODEL), F32)],
        compiler_params=_cparams(("arbitrary",), 48),
    )(dh1, h, g, dq, dk, dv, drest, wqkv, wrest)


def _loss_head(h, target, g, seq, name):
    lp = h.shape[0]
    tm = _row_tile(lp)

    def body(h_ref, t_ref, g_ref, dh_ref, loss_ref, dg_ref):
        i = pl.program_id(0)

        @pl.when(i == 0)
        def _():
            dg_ref[...] = jnp.zeros_like(dg_ref)
            loss_ref[...] = jnp.zeros_like(loss_ref)
        r, xn, y = _rms_fwd(h_ref[...], g_ref[...])
        rows = i * tm + lax.broadcasted_iota(jnp.int32, (tm, 1), 0)
        real = (rows >= N_META) & (rows < N_META + seq)
        e = jnp.where(real, y - t_ref[...], 0.0)
        loss_ref[...] += 0.5 * jnp.sum(jnp.mean(e * e, axis=-1, keepdims=True), axis=0, keepdims=True)
        dy = e * (1.0 / D_MODEL)
        dg_ref[...] += _colsum(dy * xn)
        dh_ref[...] = _rms_bwd(r, xn, g_ref[...], dy)

    row = pl.BlockSpec((tm, D_MODEL), lambda i: (i, 0))
    vec = pl.BlockSpec((1, D_MODEL), lambda i: (0, 0))
    return pl.pallas_call(
        body, name=name, grid=(lp // tm,),
        in_specs=[row, row, vec],
        out_specs=[row, pl.BlockSpec((1, 1), lambda i: (0, 0)), vec],
        out_shape=[SDS((lp, D_MODEL), F32), SDS((1, 1), F32), SDS((1, D_MODEL), F32)],
        compiler_params=_cparams(("arbitrary",), 48),
    )(h, target, g)


def _shift_down(xcat, j):
    return (xcat if j == 0 else pltpu.roll(xcat, j, 0))[HALO:, :]


def _shift_up(xcat, j, tm):
    n = xcat.shape[0]
    return (xcat if j == 0 else pltpu.roll(xcat, n - j, 0))[:tm, :]


def _layernorm_parts(x):
    mu = jnp.mean(x, axis=-1, keepdims=True)
    xc = x - mu
    rstd = lax.rsqrt(jnp.mean(xc * xc, axis=-1, keepdims=True) + EPS)
    return rstd, xc * rstd


def _log_sigmoid(z):
    return jnp.minimum(z, 0.0) - jnp.log(1.0 + jnp.exp(-jnp.abs(z)))


def _conv_fwd(rest, wdw, vec, wpw, wsc, name):
    lp = rest.shape[0]
    tm = _row_tile(lp)
    per = tm // HALO

    def body(a_ref, g_ref, ah_ref, gh_ref, b_ref, c_ref, u_ref, ch_ref, uh_ref, f_ref,
             wdw_ref, vec_ref, wpw_ref, wsc_ref, cs_ref, dwc_ref, cv_ref, cc_ref, carry):
        i = pl.program_id(0)
        first = i == 0

        @pl.when(first)
        def _():
            carry[...] = jnp.zeros_like(carry)
        glu = a_ref[...] * _sigmoid(g_ref[...])
        glu_h = jnp.where(first, 0.0, ah_ref[...] * _sigmoid(gh_ref[...]))
        xcat = jnp.concatenate([glu_h, glu], axis=0)
        acc = jnp.zeros((tm, D_CONF), F32) + vec_ref[0:1, :]
        for k in range(CONF_K):
            acc = acc + wdw_ref[k:k + 1, :] * _shift_down(xcat, CONF_K - 1 - k)
        dwc_ref[...] = acc
        _, y = _layernorm_parts(acc)
        ln = y * vec_ref[1:2, :] + vec_ref[2:3, :]
        sw = ln * _sigmoid(ln)
        conf = _dot(sw.astype(BF16), wpw_ref[...]) + vec_ref[3:4, :]
        p = c_ref[...] * u_ref[...]
        p_h = jnp.where(first, 0.0, ch_ref[...] * uh_ref[...])
        pcat = jnp.concatenate([p_h, p], axis=0)
        cv = jnp.zeros((tm, D_CONF), F32)
        for k in range(SC_K):
            cv = cv + wsc_ref[k:k + 1, :] * _shift_down(pcat, SC_K - 1 - k)
        cv_ref[...] = cv
        cs_ref[:, :D_CONF] = conf.astype(BF16)
        cs_ref[:, D_CONF:] = (b_ref[...] * cv).astype(BF16)
        logf = _log_sigmoid(f_ref[...] + vec_ref[4:5, :N_F_PAD])
        tri = (lax.broadcasted_iota(jnp.int32, (tm, tm), 0) >= lax.broadcasted_iota(jnp.int32, (tm, tm), 1))
        c = jnp.dot(tri.astype(F32), logf, precision=HIGHEST, preferred_element_type=F32) + carry[0:1, :]
        cc_ref[...] = c
        carry[0:1, :] = c[tm - 1:tm, :]

    cur = lambda j: pl.BlockSpec((tm, D_CONF), lambda i: (i, j))
    prev = lambda j: pl.BlockSpec((HALO, D_CONF), lambda i: (jnp.maximum(i * per - 1, 0), j))
    full = lambda x: pl.BlockSpec(x.shape, lambda i: (0, 0))
    return pl.pallas_call(
        body, name=name, grid=(lp // tm,),
        in_specs=[cur(0), cur(1), prev(0), prev(1), cur(2), cur(3), cur(4), prev(3), prev(4),
                  pl.BlockSpec((tm, N_F_PAD), lambda i: (i, F_BLK)),
                  full(wdw), full(vec), full(wpw), full(wsc)],
        out_specs=[pl.BlockSpec((tm, 2 * D_CONF), lambda i: (i, 0)), cur(0), cur(0),
                   pl.BlockSpec((tm, N_F_PAD), lambda i: (i, 0))],
        out_shape=[SDS((lp, 2 * D_CONF), BF16), SDS((lp, D_CONF), F32), SDS((lp, D_CONF), F32),
                   SDS((lp, N_F_PAD), F32)],
        scratch_shapes=[pltpu.VMEM((8, N_F_PAD), F32)],
        compiler_params=_cparams(("arbitrary",), 48),
    )(rest, rest, rest, rest, rest, rest, rest, rest, rest, rest, wdw, vec, wpw, wsc)


def _conv_bwd(dcs, dwc, cv, rest, dccol, wdw, vec, wpw, wsc, name):
    lp = rest.shape[0]
    tm = _row_tile(lp)
    per = tm // HALO
    nt = lp // tm
    n_halo_blocks = lp // HALO

    def body(dcf_ref, dsc_ref, dcfn_ref, dscn_ref, dwc_ref, dwcn_ref, cv_ref,
             a_ref, g_ref, ah_ref, gh_ref, b_ref, bn_ref, c_ref, u_ref, ch_ref, uh_ref, f_ref, dcc_ref,
             wdw_ref, vec_ref, wpw_ref, wsc_ref,
             dr_ref, dwdw_ref, dwsc_ref, dwpw_ref, dvec_ref, carry):
        i = pl.program_id(0)
        first = i == nt - 1
        last = i == 0

        @pl.when(i == 0)
        def _():
            carry[...] = jnp.zeros_like(carry)
            dwdw_ref[...] = jnp.zeros_like(dwdw_ref)
            dwsc_ref[...] = jnp.zeros_like(dwsc_ref)
            dwpw_ref[...] = jnp.zeros_like(dwpw_ref)
            dvec_ref[...] = jnp.zeros_like(dvec_ref)

        ln_g, ln_b = vec_ref[1:2, :], vec_ref[2:3, :]

        def ln_bwd(dconf, x):
            rstd, y = _layernorm_parts(x)
            ln = y * ln_g + ln_b
            sg = _sigmoid(ln)
            dsw = _dot_nt(dconf.astype(BF16), wpw_ref[...])
            dln = dsw * (sg * (1.0 + ln * (1.0 - sg)))
            dy = dln * ln_g
            ddw = rstd * (dy - jnp.mean(dy, axis=-1, keepdims=True) - y * jnp.mean(dy * y, axis=-1, keepdims=True))
            return ln * sg, y, dln, ddw

        dconf = dcf_ref[...]
        sw, y, dln, ddw = ln_bwd(dconf, dwc_ref[...])
        _, _, _, ddw_n = ln_bwd(dcfn_ref[...], dwcn_ref[...])
        ddw_n = jnp.where(last, 0.0, ddw_n)
        dvec_ref[0:1, :] += _colsum(ddw)
        dvec_ref[1:2, :] += _colsum(dln * y)
        dvec_ref[2:3, :] += _colsum(dln)
        dvec_ref[3:4, :] += _colsum(dconf)
        dwpw_ref[...] += _dot_tn(sw.astype(BF16), dconf.astype(BF16))

        a, sg = a_ref[...], _sigmoid(g_ref[...])
        glu = a * sg
        glu_h = jnp.where(first, 0.0, ah_ref[...] * _sigmoid(gh_ref[...]))
        xcat = jnp.concatenate([glu_h, glu], axis=0)
        dcat = jnp.concatenate([ddw, ddw_n], axis=0)
        dglu = jnp.zeros((tm, D_CONF), F32)
        for k in range(CONF_K):
            j = CONF_K - 1 - k
            dwdw_ref[k:k + 1, :] += _colsum(ddw * _shift_down(xcat, j))
            dglu = dglu + wdw_ref[k:k + 1, :] * _shift_up(dcat, j, tm)
        dr_ref[:, 0:D_CONF] = (dglu * sg).astype(BF16)
        dr_ref[:, D_CONF:2 * D_CONF] = (dglu * a * sg * (1.0 - sg)).astype(BF16)

        dsc = dsc_ref[...]
        b, c, u = b_ref[...], c_ref[...], u_ref[...]
        dcv = dsc * b
        dcv_n = jnp.where(last, 0.0, dscn_ref[...] * bn_ref[...])
        p_h = jnp.where(first, 0.0, ch_ref[...] * uh_ref[...])
        pcat = jnp.concatenate([p_h, c * u], axis=0)
        dccat = jnp.concatenate([dcv, dcv_n], axis=0)
        dp = jnp.zeros((tm, D_CONF), F32)
        for k in range(SC_K):
            j = SC_K - 1 - k
            dwsc_ref[k:k + 1, :] += _colsum(dcv * _shift_down(pcat, j))
            dp = dp + wsc_ref[k:k + 1, :] * _shift_up(dccat, j, tm)
        dr_ref[:, 2 * D_CONF:3 * D_CONF] = (dsc * cv_ref[...]).astype(BF16)
        dr_ref[:, 3 * D_CONF:4 * D_CONF] = (dp * u).astype(BF16)
        dr_ref[:, 4 * D_CONF:5 * D_CONF] = (dp * c).astype(BF16)

        tri = (lax.broadcasted_iota(jnp.int32, (tm, tm), 0) <= lax.broadcasted_iota(jnp.int32, (tm, tm), 1))
        dlogf = jnp.dot(tri.astype(F32), dcc_ref[...], precision=HIGHEST, preferred_element_type=F32) + carry[0:1, :]
        carry[0:1, :] = dlogf[0:1, :]
        dz = dlogf * _sigmoid(-(f_ref[...] + vec_ref[4:5, :N_F_PAD]))
        dvec_ref[4:5, 0:N_F_PAD] += _colsum(dz)
        dr_ref[:, 5 * D_CONF:] = dz.astype(BF16)

    rt = lambda i: nt - 1 - i
    cur = lambda j: pl.BlockSpec((tm, D_CONF), lambda i: (rt(i), j))
    prev = lambda j: pl.BlockSpec((HALO, D_CONF), lambda i: (jnp.maximum(rt(i) * per - 1, 0), j))
    nxt = lambda j: pl.BlockSpec((HALO, D_CONF), lambda i: (jnp.minimum((rt(i) + 1) * per, n_halo_blocks - 1), j))
    full = lambda x: pl.BlockSpec(x.shape, lambda i: (0, 0))
    acc = lambda r, w: pl.BlockSpec((r, w), lambda i: (0, 0))
    return pl.pallas_call(
        body, name=name, grid=(nt,),
        in_specs=[cur(0), cur(1), nxt(0), nxt(1), cur(0), nxt(0), cur(0),
                  cur(0), cur(1), prev(0), prev(1), cur(2), nxt(2), cur(3), cur(4), prev(3), prev(4),
                  pl.BlockSpec((tm, N_F_PAD), lambda i: (rt(i), F_BLK)),
                  pl.BlockSpec((tm, N_F_PAD), lambda i: (rt(i), 0)),
                  full(wdw), full(vec), full(wpw), full(wsc)],
        out_specs=[pl.BlockSpec((tm, N_REST), lambda i: (rt(i), 0)), acc(32, D_CONF), acc(8, D_CONF),
                   acc(D_CONF, D_CONF), acc(8, D_CONF)],
        out_shape=[SDS((lp, N_REST), BF16), SDS((32, D_CONF), F32), SDS((8, D_CONF), F32),
                   SDS((D_CONF, D_CONF), F32), SDS((8, D_CONF), F32)],
        scratch_shapes=[pltpu.VMEM((8, N_F_PAD), F32)],
        compiler_params=_cparams(("arbitrary",), 48),
    )(dcs, dcs, dcs, dcs, dwc, dwc, cv, rest, rest, rest, rest, rest, rest, rest, rest, rest, rest, rest, dccol,
      wdw, vec, wpw, wsc)


def _causal_tiles(n, key_major):
    if key_major:
        pairs = [(q, k) for k in range(n) for q in range(k, n)]
    else:
        pairs = [(q, k) for q in range(n) for k in range(q + 1)]
    return (jnp.asarray([p[0] for p in pairs], jnp.int32), jnp.asarray([p[1] for p in pairs], jnp.int32))


def _attn_fwd(qkv, crow, cpair, name):
    lp = qkv.shape[0]
    t = _row_tile(lp)
    nq = lp // t

    qtab, ktab = _causal_tiles(nq, key_major=False)

    def body(qt_ref, kt_ref, q_ref, k_ref, v_ref, c_ref, cq_ref, o_ref, lse_ref, m_scr, l_scr, acc_scr):
        qi, ki = qt_ref[pl.program_id(1)], kt_ref[pl.program_id(1)]

        @pl.when(ki == 0)
        def _():
            lane = lax.broadcasted_iota(jnp.int32, (t, PAIR_W), 1)
            m_scr[...] = jnp.full_like(m_scr, NEG)
            l_scr[...] = jnp.where(lane < 2, 0.0, 1.0)
            acc_scr[...] = jnp.zeros_like(acc_scr)

        def step(diagonal):
            q, k, v = q_ref[...], k_ref[...], v_ref[...]
            lane = lax.broadcasted_iota(jnp.int32, (t, PAIR_W), 1)
            lo = lane < HEAD_DIM
            m_all, l_all = m_scr[...], l_scr[...]
            pv, alpha, m_out, l_out = [], [], [], []
            for j in range(2):
                sel = lo if j == 0 else jnp.logical_not(lo)
                s = _dot_nt(jnp.where(sel, q, jnp.zeros_like(q)), k) + (cq_ref[:, j:j + 1] - c_ref[j:j + 1, :])
                if diagonal:
                    causal = (lax.broadcasted_iota(jnp.int32, (t, t), 0) >= lax.broadcasted_iota(jnp.int32, (t, t), 1))
                    s = jnp.where(causal, s, -jnp.inf)
                m_prev = m_all[:, j:j + 1]
                m_new = jnp.maximum(m_prev, jnp.max(s, axis=1, keepdims=True))
                al = jnp.exp(m_prev - m_new)
                p = jnp.exp(s - m_new)
                l_out.append(al * l_all[:, j:j + 1] + jnp.sum(p, axis=1, keepdims=True))
                m_out.append(m_new)
                pv.append(_dot(p.astype(BF16), v))
                alpha.append(al)
            m_scr[...] = jnp.where(lane == 0, m_out[0], jnp.where(lane == 1, m_out[1], m_all))
            l_scr[...] = jnp.where(lane == 0, l_out[0], jnp.where(lane == 1, l_out[1], l_all))
            acc_scr[...] = acc_scr[...] * jnp.where(lo, alpha[0], alpha[1]) + jnp.where(lo, pv[0], pv[1])

        @pl.when(ki < qi)
        def _():
            step(False)

        @pl.when(ki == qi)
        def _():
            step(True)
            lo = lax.broadcasted_iota(jnp.int32, (t, PAIR_W), 1) < HEAD_DIM
            inv = jnp.where(lo, 1.0 / l_scr[:, 0:1], 1.0 / l_scr[:, 1:2])
            o_ref[...] = (acc_scr[...] * inv).astype(BF16)
            lse_ref[...] = m_scr[...] + jnp.log(l_scr[...])

    kv = lambda off: pl.BlockSpec((t, PAIR_W), lambda p, s, qt, kt: (kt[s], off + p))
    qblk = pl.BlockSpec((t, PAIR_W), lambda p, s, qt, kt: (qt[s], p))
    return pl.pallas_call(
        body, name=name,
        grid_spec=pltpu.PrefetchScalarGridSpec(
            num_scalar_prefetch=2, grid=(N_PAIRS, qtab.shape[0]),
            in_specs=[qblk, kv(N_PAIRS), kv(2 * N_PAIRS),
                      pl.BlockSpec((None, 2, t), lambda p, s, qt, kt: (p, 0, kt[s])), qblk],
            out_specs=[qblk, qblk],
            scratch_shapes=[pltpu.VMEM((t, PAIR_W), F32), pltpu.VMEM((t, PAIR_W), F32), pltpu.VMEM((t, PAIR_W), F32)]),
        out_shape=[SDS((lp, D_ATTN), BF16), SDS((lp, D_ATTN), F32)],
        compiler_params=_cparams(("parallel", "arbitrary"), 48),
    )(qtab, ktab, qkv, qkv, qkv, crow, cpair)


def _attn_bwd(qkv, do, crow, cpair, lse, delta, name):
    lp = qkv.shape[0]
    t = _row_tile(lp)
    nq = lp // t
    qtab, ktab = _causal_tiles(nq, key_major=True)

    def body(qt_ref, kt_ref, q_ref, k_ref, v_ref, do_ref, c_ref, cq_ref, lse_ref, dl_ref,
             dq_ref, dk_ref, dv_ref, dc_ref, dcq_ref, dq_scr, dcq_scr, dk_scr, dv_scr, dc_scr):
        step_id = pl.program_id(1)
        qi, ki = qt_ref[step_id], kt_ref[step_id]

        @pl.when(step_id == 0)
        def _():
            dq_scr[...] = jnp.zeros_like(dq_scr)
            dcq_scr[...] = jnp.zeros_like(dcq_scr)

        @pl.when(qi == ki)
        def _():
            dk_scr[...] = jnp.zeros_like(dk_scr)
            dv_scr[...] = jnp.zeros_like(dv_scr)
            dc_scr[...] = jnp.zeros_like(dc_scr)

        def step(diagonal):
            q, k, v, dout = q_ref[...], k_ref[...], v_ref[...], do_ref[...]
            lane = lax.broadcasted_iota(jnp.int32, (t, PAIR_W), 1)
            lo = lane < HEAD_DIM
            zero = jnp.zeros_like(q)
            dq_new = jnp.zeros((t, PAIR_W), F32)
            dv_new = jnp.zeros((t, PAIR_W), F32)
            dk_new = jnp.zeros((t, PAIR_W), F32)
            col_sums, row_sums = [], []
            for j in range(2):
                sel = lo if j == 0 else jnp.logical_not(lo)
                qj, kj, doj = jnp.where(sel, q, zero), jnp.where(sel, k, zero), jnp.where(sel, dout, zero)
                p = jnp.exp(_dot_nt(qj, k) + (cq_ref[:, j:j + 1] - c_ref[j:j + 1, :]) - lse_ref[:, j:j + 1])
                if diagonal:
                    causal = (lax.broadcasted_iota(jnp.int32, (t, t), 0) >= lax.broadcasted_iota(jnp.int32, (t, t), 1))
                    p = jnp.where(causal, p, 0.0)
                dp = _dot_nt(doj, v)
                ds = p * (dp - dl_ref[:, j:j + 1])
                col_sums.append(_colsum(ds))
                row_sums.append(jnp.sum(ds, axis=1, keepdims=True))
                dsb = ds.astype(BF16)
                dv_new = dv_new + _dot_tn(p.astype(BF16), doj)
                dk_new = dk_new + _dot_tn(dsb, qj)
                dq_new = dq_new + _dot(dsb, kj)
            dv_scr[...] += dv_new
            dk_scr[...] += dk_new
            dc_scr[0:2, :] -= jnp.concatenate(col_sums, axis=0)
            rows = pl.ds(pl.multiple_of(qi * t, t), t)
            dq_scr[rows, :] += dq_new
            dcq_scr[rows, :] += jnp.where(lane == 0, row_sums[0], jnp.where(lane == 1, row_sums[1], 0.0))

        @pl.when(qi > ki)
        def _():
            step(False)

        @pl.when(qi == ki)
        def _():
            step(True)
            rows = pl.ds(pl.multiple_of(qi * t, t), t)
            dq_ref[...] = (dq_scr[rows, :] * (HEAD_DIM ** -0.5)).astype(BF16)
            dcq_ref[...] = dcq_scr[rows, :]

        @pl.when(qi == nq - 1)
        def _():
            dk_ref[...] = dk_scr[...].astype(BF16)
            dv_ref[...] = dv_scr[...].astype(BF16)
            dc_ref[...] = dc_scr[0:2, :]

    qside = lambda off: pl.BlockSpec((t, PAIR_W), lambda p, s, qt, kt: (qt[s], off + p))
    kside = lambda off: pl.BlockSpec((t, PAIR_W), lambda p, s, qt, kt: (kt[s], off + p))
    crow_blk = pl.BlockSpec((None, 2, t), lambda p, s, qt, kt: (p, 0, kt[s]))
    return pl.pallas_call(
        body, name=name,
        grid_spec=pltpu.PrefetchScalarGridSpec(
            num_scalar_prefetch=2, grid=(N_PAIRS, qtab.shape[0]),
            in_specs=[qside(0), kside(N_PAIRS), kside(2 * N_PAIRS), qside(0), crow_blk, qside(0), qside(0), qside(0)],
            out_specs=[kside(0), kside(0), kside(0), crow_blk, kside(0)],
            scratch_shapes=[pltpu.VMEM((lp, PAIR_W), F32), pltpu.VMEM((lp, PAIR_W), F32), pltpu.VMEM((t, PAIR_W), F32),
                            pltpu.VMEM((t, PAIR_W), F32), pltpu.VMEM((8, t), F32)]),
        out_shape=[SDS((lp, D_ATTN), BF16), SDS((lp, D_ATTN), BF16), SDS((lp, D_ATTN), BF16),
                   SDS((N_PAIRS, 2, lp), F32), SDS((lp, D_ATTN), F32)],
        compiler_params=_cparams(("parallel", "arbitrary"), 48),
    )(qtab, ktab, qkv, qkv, qkv, do, crow, cpair, lse, delta)


def _shard_sum(own, parts, name, slot_order=False):
    rows, cols = own.shape
    tr = rows if rows <= 512 else 512
    assert rows % tr == 0

    def body(me_ref, own_ref, parts_ref, o_ref):
        if slot_order:
            acc = parts_ref[0].astype(F32)
            for j in range(1, 4):
                acc = acc + parts_ref[j].astype(F32)
        else:
            me = me_ref[0]
            acc = own_ref[...]
            for j in range(4):
                acc = acc + jnp.where(me == j, 0.0, parts_ref[j].astype(F32))
        o_ref[...] = acc

    me = (2 * lax.axis_index("x") + lax.axis_index("y")).astype(jnp.int32).reshape(1)
    return pl.pallas_call(
        body, name=name,
        grid_spec=pltpu.PrefetchScalarGridSpec(
            num_scalar_prefetch=1, grid=(rows // tr,),
            in_specs=[pl.BlockSpec((tr, cols), lambda i, me: (i, 0)), pl.BlockSpec((4, tr, cols), lambda i, me: (0, i, 0))],
            out_specs=pl.BlockSpec((tr, cols), lambda i, me: (i, 0))),
        out_shape=SDS((rows, cols), F32),
        compiler_params=_cparams(("parallel",), 48),
    )(me, own, parts)


def _adamw(w, m, v, s_own, s_sib, name):
    rows, cols = w.shape
    tr = rows if rows <= 256 else 256
    assert rows % tr == 0

    def body(w_ref, m_ref, v_ref, a_ref, b_ref, g_ref, d_ref, nm_ref, nv_ref):
        g = a_ref[...] + b_ref[...]
        g_ref[...] = g
        m = ADAM_B1 * m_ref[...] + (1.0 - ADAM_B1) * g
        v = ADAM_B2 * v_ref[...] + (1.0 - ADAM_B2) * jnp.square(g)
        nm_ref[...] = m
        nv_ref[...] = v
        m_hat = m / (1.0 - ADAM_B1 ** ADAM_STEP)
        v_hat = v / (1.0 - ADAM_B2 ** ADAM_STEP)
        d_ref[...] = -ADAM_LR * (m_hat / (jnp.sqrt(v_hat) + ADAM_EPS) + ADAM_WD * w_ref[...])

    blk = pl.BlockSpec((tr, cols), lambda i: (i, 0))
    return pl.pallas_call(
        body, name=name, grid=(rows // tr,), in_specs=[blk] * 5, out_specs=[blk] * 4,
        out_shape=[SDS((rows, cols), F32)] * 4,
        compiler_params=_cparams(("parallel",), 48),
    )(w, m, v, s_own, s_sib)


def _pad_rows(a, rows):
    return jnp.pad(a, ((0, rows - a.shape[0]), (0, 0)))


def _flat_rows(arrs, cols, rows):
    flat = jnp.concatenate([a.reshape(-1) for a in arrs])
    return jnp.pad(flat, (0, rows * cols - flat.shape[0])).reshape(rows, cols)


def _unflat(buf, shapes):
    flat, out, off = buf.reshape(-1), [], 0
    for s in shapes:
        n = 1
        for d in s:
            n *= d
        out.append(flat[off:off + n].reshape(s))
        off += n
    return out


def kernel(x, meta_tokens, mix_norm_g, w_in, b_forget, w_conf_dw, b_conf_dw, conf_ln_g, conf_ln_b, w_conf_pw, b_conf_pw, w_sc_conv, w_out, mlp_norm_g, w_mlp1, w_mlp2, final_norm_g, loss_target, m_meta_tokens, m_mix_norm_g, m_w_in, m_b_forget, m_w_conf_dw, m_b_conf_dw, m_conf_ln_g, m_conf_ln_b, m_w_conf_pw, m_b_conf_pw, m_w_sc_conv, m_w_out, m_mlp_norm_g, m_w_mlp1, m_w_mlp2, m_final_norm_g, v_meta_tokens, v_mix_norm_g, v_w_in, v_b_forget, v_w_conf_dw, v_b_conf_dw, v_conf_ln_g, v_conf_ln_b, v_w_conf_pw, v_b_conf_pw, v_w_sc_conv, v_w_out, v_mlp_norm_g, v_w_mlp1, v_w_mlp2, v_final_norm_g):
    depth = w_in.shape[0]
    seq = x.shape[1]
    l_tok = N_META + seq
    lp = _padded_len(l_tok)
    n_in = w_in.shape[2] * 4
    n_in_sh = w_in.shape[2]
    me = 2 * lax.axis_index("x") + lax.axis_index("y")

    small_shapes = [w_conf_dw.shape, w_conf_pw.shape, w_sc_conv.shape, meta_tokens.shape]
    n_small = sum(a.size for a in (w_conf_dw, w_conf_pw, w_sc_conv, meta_tokens))
    small_rows = -(-n_small // 1024 // 8) * 8
    small = _flat_rows([w_conf_dw, w_conf_pw, w_sc_conv, meta_tokens], 1024, small_rows)
    g_in, g_out, g_m1, g_m2, g_small = _weights_all_gather(
        [w_in.astype(BF16).reshape(depth * D_MODEL, n_in_sh), w_out.astype(BF16).reshape(-1, D_MODEL),
         w_mlp1.astype(BF16).reshape(depth * D_MODEL, -1), w_mlp2.astype(BF16).reshape(-1, D_MODEL), small],
        name="weights_all_gather")
    w_in_f = jnp.concatenate([g_in[j].reshape(depth, D_MODEL, n_in_sh) for j in range(4)], axis=2)
    w_out_f = jnp.concatenate([g_out[j].reshape(depth, -1, D_MODEL) for j in range(4)], axis=1)
    w_m1_f = jnp.concatenate([g_m1[j].reshape(depth, D_MODEL, -1) for j in range(4)], axis=2)
    w_m2_f = jnp.concatenate([g_m2[j].reshape(depth, -1, D_MODEL) for j in range(4)], axis=1)
    sm = [_unflat(g_small[j], small_shapes) for j in range(4)]
    w_dw_f = jnp.concatenate([s[0] for s in sm], axis=2)
    w_pw_f = jnp.concatenate([s[1] for s in sm], axis=1)
    w_sc_f = jnp.concatenate([s[2] for s in sm], axis=2)
    meta_f = jnp.concatenate([s[3] for s in sm], axis=1)

    q_end, f_end = 3 * D_ATTN, 3 * D_ATTN + N_HEADS
    wqkv = [w_in_f[l, :, :q_end] for l in range(depth)]
    wrest = [jnp.concatenate([w_in_f[l, :, f_end:], w_in_f[l, :, q_end:f_end],
                              jnp.zeros((D_MODEL, N_F_PAD - N_HEADS), BF16)], axis=1) for l in range(depth)]
    wdw = [_pad_rows(w_dw_f[l], 32) for l in range(depth)]
    wsc = [_pad_rows(w_sc_f[l], 8) for l in range(depth)]
    wpw = [w_pw_f[l].astype(BF16) for l in range(depth)]
    vec = [_pad_rows(jnp.stack([b_conf_dw[l], conf_ln_g[l], conf_ln_b[l], b_conf_pw[l],
                                jnp.pad(b_forget[l], (0, D_CONF - N_HEADS))]), 8) for l in range(depth)]
    col = jnp.arange(D_ATTN)
    seg = (col[None, :] == ((col // PAIR_W) * PAIR_W + (col % PAIR_W) // HEAD_DIM)[:, None]).astype(F32)

    h = jnp.concatenate([meta_f, x[0], jnp.zeros((lp - l_tok, D_MODEL), F32)], axis=0)
    target = jnp.pad(loss_target[0], ((N_META, lp - l_tok), (0, 0)))
    saved = []
    for l in range(depth):
        g1, g2 = mix_norm_g[l][None, :], mlp_norm_g[l][None, :]
        hn1, qkv, rest = _proj_fwd(h, g1, wqkv[l], wrest[l], name=f"proj_fwd_{l}")
        cs, dwc, cv, ccol = _conv_fwd(rest, wdw[l], vec[l], wpw[l], wsc[l], name=f"conv_fwd_{l}")
        crow = ccol[:, :N_HEADS].T.reshape(N_PAIRS, 2, lp)
        cpair = jnp.pad(ccol[:, :N_HEADS].reshape(lp, N_PAIRS, 2), ((0, 0), (0, 0), (0, PAIR_W - 2))).reshape(lp, D_ATTN)
        attn, lse = _attn_fwd(qkv, crow, cpair, name=f"attn_fwd_{l}")
        h1 = _mix_out(h, attn, cs, w_out_f[l, :D_ATTN], w_out_f[l, D_ATTN:], name=f"mix_out_{l}")
        hn2, a, r, h2 = _mlp_fwd(h1, g2, w_m1_f[l], w_m2_f[l], name=f"mlp_fwd_{l}")
        saved.append((h, g1, g2, hn1, qkv, rest, cs, dwc, cv, crow, cpair, attn, lse, h1, hn2, a, r))
        h = h2
    dh, loss11, d_final_g = _loss_head(h, target, final_norm_g[None, :], seq, name="loss_head")
    loss = lax.psum(loss11[0, 0], ("x", "y", "c"))

    grads = [None] * depth
    for l in reversed(range(depth)):
        h0, g1, g2, hn1, qkv, rest, cs, dwc, cv, crow, cpair, attn, lse, h1, hn2, a, r = saved[l]
        da, dh1, dg2 = _mlp_bwd(dh, h1, g2, a, w_m1_f[l], w_m2_f[l], name=f"mlp_bwd_{l}")
        d_w2 = _mm_tn(r, dh, name=f"dw_mlp2_{l}")
        d_w1 = _mm_tn(hn2, da, name=f"dw_mlp1_{l}")
        do, dcs, delta = _mix_bwd(dh1, w_out_f[l, :D_ATTN], w_out_f[l, D_ATTN:], attn, seg, name=f"mix_bwd_{l}")
        d_wout = jnp.concatenate([_mm_tn(attn, dh1, name=f"dw_out_a_{l}"), _mm_tn(cs, dh1, name=f"dw_out_c_{l}")], axis=0)
        dq, dk, dv, dcrow, dcq = _attn_bwd(qkv, do, crow, cpair, lse, delta, name=f"attn_bwd_{l}")
        dc_heads = dcrow.reshape(N_HEADS, lp).T + dcq.reshape(lp, N_PAIRS, PAIR_W)[:, :, :2].reshape(lp, N_HEADS)
        dccol = jnp.pad(dc_heads, ((0, 0), (0, N_F_PAD - N_HEADS)))
        drest, d_wdw, d_wsc, d_wpw, d_vec = _conv_bwd(dcs, dwc, cv, rest, dccol, wdw[l], vec[l], wpw[l], wsc[l],
                                                      name=f"conv_bwd_{l}")
        dh, dg1 = _in_bwd(dh1, h0, g1, dq, dk, dv, drest, wqkv[l], wrest[l], name=f"in_bwd_{l}")
        d_wq = _mm_tn(hn1, dq, name=f"dw_q_{l}")
        d_wk = _mm_tn(hn1, dk, name=f"dw_k_{l}")
        d_wv = _mm_tn(hn1, dv, name=f"dw_v_{l}")
        d_wr = _mm_tn(hn1, drest, name=f"dw_rest_{l}")
        d_win = jnp.concatenate([d_wq, d_wk, d_wv, d_wr[:, 5 * D_CONF:5 * D_CONF + N_HEADS], d_wr[:, :5 * D_CONF]], axis=1)
        grads[l] = dict(w_in=d_win, w_out=d_wout, w_mlp1=d_w1, w_mlp2=d_w2, mix_norm_g=dg1[0], mlp_norm_g=dg2[0],
                        w_conf_dw=d_wdw[:CONF_K], w_sc_conv=d_wsc[:SC_K], w_conf_pw=d_wpw, b_conf_dw=d_vec[0],
                        conf_ln_g=d_vec[1], conf_ln_b=d_vec[2], b_conf_pw=d_vec[3], b_forget=d_vec[4, :N_HEADS])
    grad_x = dh[N_META:N_META + seq][None]
    d_meta = dh[:N_META]

    def stack(k):
        return jnp.stack([grads[l][k] for l in range(depth)])

    def col_shards(a):
        d, rr, cc = a.shape
        return a.reshape(d, rr, 4, cc // 4).transpose(2, 0, 1, 3).reshape(4, d * rr, cc // 4)

    def row_shards(a):
        d, rr, cc = a.shape
        return a.reshape(d, 4, rr // 4, cc).transpose(1, 0, 2, 3).reshape(4, d * rr // 4, cc)

    big = [col_shards(stack("w_in")), row_shards(stack("w_out")), col_shards(stack("w_mlp1")), row_shards(stack("w_mlp2"))]
    rep = [stack("mix_norm_g"), stack("b_forget"), stack("b_conf_dw"), stack("conf_ln_g"), stack("conf_ln_b"),
           stack("b_conf_pw"), stack("mlp_norm_g"), d_final_g[0]]
    rep_shapes = [a.shape for a in rep]
    sh_dw, sh_pw, sh_sc = col_shards(stack("w_conf_dw")), row_shards(stack("w_conf_pw")), col_shards(stack("w_sc_conv"))
    sh_meta = d_meta.reshape(N_META, 4, D_MODEL // 4).transpose(1, 0, 2)
    n_rep = sum(a.size for a in rep)
    gs_rows = -(-(n_small + n_rep) // 1024 // 8) * 8
    gsmall = jnp.stack([_flat_rows([sh_dw[j], sh_pw[j], sh_sc[j], sh_meta[j]] + rep, 1024, gs_rows) for j in range(4)])
    recv = _chip_exchange([b.astype(BF16) for b in big] + [gsmall], name="grads_exchange")
    own = [lax.dynamic_index_in_dim(b, me, 0, keepdims=False) for b in big + [gsmall]]
    sums = [_shard_sum(own[i], recv[i], name=f"shard_sum_{i}", slot_order=(i == 4)) for i in range(5)]
    sibs = _sibling_exchange(sums, name="sibling_exchange")

    w_small = _flat_rows([w_conf_dw, w_conf_pw, w_sc_conv, meta_tokens, mix_norm_g, b_forget, b_conf_dw, conf_ln_g,
                          conf_ln_b, b_conf_pw, mlp_norm_g, final_norm_g], 1024, gs_rows)
    m_small = _flat_rows([m_w_conf_dw, m_w_conf_pw, m_w_sc_conv, m_meta_tokens, m_mix_norm_g, m_b_forget, m_b_conf_dw,
                          m_conf_ln_g, m_conf_ln_b, m_b_conf_pw, m_mlp_norm_g, m_final_norm_g], 1024, gs_rows)
    v_small = _flat_rows([v_w_conf_dw, v_w_conf_pw, v_w_sc_conv, v_meta_tokens, v_mix_norm_g, v_b_forget, v_b_conf_dw,
                          v_conf_ln_g, v_conf_ln_b, v_b_conf_pw, v_mlp_norm_g, v_final_norm_g], 1024, gs_rows)
    triples = [(w_in, m_w_in, v_w_in), (w_out, m_w_out, v_w_out), (w_mlp1, m_w_mlp1, v_w_mlp1),
               (w_mlp2, m_w_mlp2, v_w_mlp2), (w_small, m_small, v_small)]
    res = []
    for i, (w, m, v) in enumerate(triples):
        shp = w.shape
        two_d = sums[i].shape
        out4 = _adamw(w.reshape(two_d), m.reshape(two_d), v.reshape(two_d), sums[i], sibs[i], name=f"adamw_{i}")
        res.append([o.reshape(shp) for o in out4])
    small_out = [_unflat(res[4][k], small_shapes + rep_shapes) for k in range(4)]

    names = ["meta_tokens", "mix_norm_g", "w_in", "b_forget", "w_conf_dw", "b_conf_dw", "conf_ln_g", "conf_ln_b",
             "w_conf_pw", "b_conf_pw", "w_sc_conv", "w_out", "mlp_norm_g", "w_mlp1", "w_mlp2", "final_norm_g"]
    small_order = ["w_conf_dw", "w_conf_pw", "w_sc_conv", "meta_tokens", "mix_norm_g", "b_forget", "b_conf_dw",
                   "conf_ln_g", "conf_ln_b", "b_conf_pw", "mlp_norm_g", "final_norm_g"]
    big_order = {"w_in": 0, "w_out": 1, "w_mlp1": 2, "w_mlp2": 3}
    outs = []
    for k in range(4):
        for nme in names:
            outs.append(res[big_order[nme]][k] if nme in big_order else small_out[k][small_order.index(nme)])
    return (loss, grad_x, *outs)
```

```python
import jax
import jax.numpy as jnp
from jax import lax
from jax.experimental import pallas as pl
from jax.experimental.pallas import tpu as pltpu

F32, BF16 = jnp.float32, jnp.bfloat16
D_MODEL = 1024
D_ATTN = 512
D_CONF = 256
N_HEADS = 8
HEAD_DIM = 64
N_PAIRS = N_HEADS // 2
PAIR_W = 2 * HEAD_DIM
N_META = 16
D_FF = 4096
CONF_K = 31
SC_K = 3
HALO = 32
N_F_PAD = 128
N_REST = 5 * D_CONF + N_F_PAD
F_BLK = 5 * D_CONF // N_F_PAD
EPS = 1e-6
NEG = -1e30
BIG_TILE = 640
SMALL_TILE = 128
MLP_TILE_DIV = 2
V7X_VMEM_BYTES = 64 * 1024 * 1024
ADAM_LR, ADAM_B1, ADAM_B2, ADAM_EPS, ADAM_WD, ADAM_STEP = 0.001, 0.9, 0.999, 1e-08, 0.01, 10
MESH = pl.DeviceIdType.MESH
HIGHEST = lax.Precision.HIGHEST
SDS = jax.ShapeDtypeStruct


def _cparams(sem=None, vmem_mb=None):
    kw = {}
    if sem is not None:
        kw["dimension_semantics"] = sem
    if vmem_mb is not None:
        assert vmem_mb * 1024 * 1024 <= V7X_VMEM_BYTES
        kw["vmem_limit_bytes"] = vmem_mb * 1024 * 1024
    return pltpu.CompilerParams(**kw)


def _dot(a, b):
    return jnp.dot(a, b, preferred_element_type=F32)


def _dot_nt(a, b):
    return lax.dot_general(a, b, (((1,), (1,)), ((), ())), preferred_element_type=F32)


def _dot_tn(a, b):
    return lax.dot_general(a, b, (((0,), (0,)), ((), ())), preferred_element_type=F32)


def _sigmoid(x):
    return 1.0 / (1.0 + jnp.exp(-x))


def _rms_fwd(x, g):
    r = lax.rsqrt(jnp.mean(x * x, axis=-1, keepdims=True) + EPS)
    xn = x * r
    return r, xn, xn * g


def _rms_bwd(r, xn, g, dy):
    u = dy * g
    return r * (u - xn * jnp.mean(u * xn, axis=-1, keepdims=True))


def _colsum(x):
    return jnp.sum(x, axis=0, keepdims=True)


def _row_tile(lp):
    return BIG_TILE if lp % BIG_TILE == 0 else SMALL_TILE


def _padded_len(l):
    t = BIG_TILE if l >= BIG_TILE else SMALL_TILE
    return -(-l // t) * t


def _weights_all_gather(arrs, name):
    n = len(arrs)
    halves = [a.shape[0] // 2 for a in arrs]
    assert all(a.shape[0] == 2 * h and h % 8 == 0 for a, h in zip(arrs, halves))

    def body(*refs):
        ins, outs = refs[:n], refs[n:2 * n]
        ici_send, ici_recv, d2d_send, d2d_recv, local_sems = refs[2 * n:]
        x, y, c = lax.axis_index("x"), lax.axis_index("y"), lax.axis_index("c")
        me = 2 * x + y
        sibling = (x, y, 1 - c)
        chips = [(1 - x, y), (x, 1 - y), (1 - x, 1 - y)]

        def half(ref, i, which):
            return ref.at[pl.ds(pl.multiple_of(which * halves[i], 8), halves[i])]

        def copy(src, dst, send_sem, recv_sem, to):
            return pltpu.make_async_remote_copy(src_ref=src, dst_ref=dst, send_sem=send_sem, recv_sem=recv_sem,
                                                device_id=to, device_id_type=MESH)

        local, over_ici, landed, passed_on, from_sibling = [], [], [], [], []
        for i in range(n):
            local.append(pltpu.make_async_copy(ins[i], outs[i].at[me], local_sems.at[i]))
            for j, (px, py) in enumerate(chips):
                k = 3 * i + j
                peer = 2 * px + py
                mine, theirs = half(outs[i].at[me], i, c), half(outs[i].at[peer], i, c)
                over_ici.append(copy(half(ins[i], i, c), mine, ici_send.at[k], ici_recv.at[k], (px, py, c)))
                landed.append(copy(half(ins[i], i, c), theirs, ici_send.at[k], ici_recv.at[k], (px, py, c)))
                passed_on.append(copy(theirs, theirs, d2d_send.at[k], d2d_recv.at[k], sibling))
                other = half(outs[i].at[peer], i, 1 - c)
                from_sibling.append(copy(other, other, d2d_send.at[k], d2d_recv.at[k], sibling))
        for cp in local + over_ici:
            cp.start()
        for arrived, onward in zip(landed, passed_on):
            arrived.wait_recv()
            onward.start()
        for cp in from_sibling:
            cp.wait_recv()
        for cp in over_ici + passed_on:
            cp.wait_send()
        for cp in local:
            cp.wait()

    any_spec = pl.BlockSpec(memory_space=pl.ANY)
    return pl.pallas_call(
        body, name=name, out_shape=[SDS((4,) + a.shape, a.dtype) for a in arrs],
        in_specs=[any_spec] * n, out_specs=[any_spec] * n,
        scratch_shapes=[pltpu.SemaphoreType.DMA((3 * n,)) for _ in range(4)] + [pltpu.SemaphoreType.DMA((n,))],
    )(*arrs)


def _chip_exchange(arrs, name):
    n = len(arrs)

    def body(*refs):
        ins, outs = refs[:n], refs[n:2 * n]
        send_sems, recv_sems, local_sems = refs[2 * n:]
        x, y, c = lax.axis_index("x"), lax.axis_index("y"), lax.axis_index("c")
        me = 2 * x + y
        chips = [(1 - x, y), (x, 1 - y), (1 - x, 1 - y)]
        local, sends, recvs = [], [], []
        for i in range(n):
            local.append(pltpu.make_async_copy(ins[i].at[me], outs[i].at[me], local_sems.at[i]))
            for j, (px, py) in enumerate(chips):
                k = 3 * i + j
                peer = 2 * px + py
                src = ins[i].at[peer]
                sends.append(pltpu.make_async_remote_copy(
                    src_ref=src, dst_ref=outs[i].at[me], send_sem=send_sems.at[k], recv_sem=recv_sems.at[k],
                    device_id=(px, py, c), device_id_type=MESH))
                recvs.append(pltpu.make_async_remote_copy(
                    src_ref=src, dst_ref=outs[i].at[peer], send_sem=send_sems.at[k], recv_sem=recv_sems.at[k],
                    device_id=(px, py, c), device_id_type=MESH))
        for cp in local + sends:
            cp.start()
        for cp in recvs:
            cp.wait_recv()
        for cp in sends:
            cp.wait_send()
        for cp in local:
            cp.wait()

    any_spec = pl.BlockSpec(memory_space=pl.ANY)
    return pl.pallas_call(
        body, name=name, out_shape=[SDS(a.shape, a.dtype) for a in arrs],
        in_specs=[any_spec] * n, out_specs=[any_spec] * n,
        scratch_shapes=[pltpu.SemaphoreType.DMA((3 * n,)), pltpu.SemaphoreType.DMA((3 * n,)),
                        pltpu.SemaphoreType.DMA((n,))],
    )(*arrs)


def _sibling_exchange(arrs, name):
    n = len(arrs)

    def body(*refs):
        ins, outs = refs[:n], refs[n:2 * n]
        send_sems, recv_sems = refs[2 * n:]
        sib = (lax.axis_index("x"), lax.axis_index("y"), 1 - lax.axis_index("c"))
        cps = [pltpu.make_async_remote_copy(src_ref=ins[i], dst_ref=outs[i], send_sem=send_sems.at[i],
                                            recv_sem=recv_sems.at[i], device_id=sib, device_id_type=MESH)
               for i in range(n)]
        for cp in cps:
            cp.start()
        for cp in cps:
            cp.wait_recv()
        for cp in cps:
            cp.wait_send()

    any_spec = pl.BlockSpec(memory_space=pl.ANY)
    return pl.pallas_call(
        body, name=name, out_shape=[SDS(a.shape, a.dtype) for a in arrs],
        in_specs=[any_spec] * n, out_specs=[any_spec] * n,
        scratch_shapes=[pltpu.SemaphoreType.DMA((n,)), pltpu.SemaphoreType.DMA((n,))],
    )(*arrs)


def _proj_fwd(h, g, wqkv, wrest, name):
    lp = h.shape[0]
    tm = _row_tile(lp)

    def body(h_ref, g_ref, wq_ref, wr_ref, hn_ref, qkv_ref, rest_ref):
        _, _, y = _rms_fwd(h_ref[...], g_ref[...])
        hn = y.astype(BF16)
        hn_ref[...] = hn
        qkv = _dot(hn, wq_ref[...])
        qkv_ref[:, :D_ATTN] = (qkv[:, :D_ATTN] * (HEAD_DIM ** -0.5)).astype(BF16)
        qkv_ref[:, D_ATTN:] = qkv[:, D_ATTN:].astype(BF16)
        rest_ref[...] = _dot(hn, wr_ref[...])

    row = lambda w: pl.BlockSpec((tm, w), lambda i: (i, 0))
    full = lambda a: pl.BlockSpec(a.shape, lambda i: (0, 0))
    return pl.pallas_call(
        body, name=name, grid=(lp // tm,),
        in_specs=[row(D_MODEL), full(g), full(wqkv), full(wrest)],
        out_specs=[row(D_MODEL), row(3 * D_ATTN), row(N_REST)],
        out_shape=[SDS((lp, D_MODEL), BF16), SDS((lp, 3 * D_ATTN), BF16), SDS((lp, N_REST), F32)],
        compiler_params=_cparams(("parallel",), 48),
    )(h, g, wqkv, wrest)


def _mm_tn(a, b, name):
    kk, m = a.shape
    n = b.shape[1]
    tk = _row_tile(kk)
    tmo = min(m, 1024)
    tno = n if n <= 1536 else 1024

    def body(a_ref, b_ref, o_ref):
        @pl.when(pl.program_id(2) == 0)
        def _():
            o_ref[...] = jnp.zeros_like(o_ref)
        o_ref[...] += _dot_tn(a_ref[...], b_ref[...].astype(BF16))

    return pl.pallas_call(
        body, name=name, grid=(m // tmo, n // tno, kk // tk),
        in_specs=[pl.BlockSpec((tk, tmo), lambda i, j, k: (k, i)), pl.BlockSpec((tk, tno), lambda i, j, k: (k, j))],
        out_specs=pl.BlockSpec((tmo, tno), lambda i, j, k: (i, j)),
        out_shape=SDS((m, n), F32),
        compiler_params=_cparams(("parallel", "parallel", "arbitrary"), 48),
    )(a, b)


def _mix_out(h, attn, cs, wa, wc, name):
    lp = h.shape[0]
    tm = _row_tile(lp)

    def body(h_ref, at_ref, cs_ref, wa_ref, wc_ref, o_ref):
        o_ref[...] = h_ref[...] + _dot(at_ref[...], wa_ref[...]) + _dot(cs_ref[...], wc_ref[...])

    row = lambda w: pl.BlockSpec((tm, w), lambda i: (i, 0))
    full = lambda a: pl.BlockSpec(a.shape, lambda i: (0, 0))
    return pl.pallas_call(
        body, name=name, grid=(lp // tm,),
        in_specs=[row(D_MODEL), row(D_ATTN), row(D_ATTN), full(wa), full(wc)],
        out_specs=row(D_MODEL), out_shape=SDS((lp, D_MODEL), F32),
        compiler_params=_cparams(("parallel",), 48),
    )(h, attn, cs, wa, wc)


def _mlp_fwd(h, g, w1, w2, name):
    lp = h.shape[0]
    tm = _row_tile(lp) // MLP_TILE_DIV

    def body(h_ref, g_ref, w1_ref, w2_ref, hn_ref, a_ref, r_ref, o_ref):
        x = h_ref[...]
        _, _, y = _rms_fwd(x, g_ref[...])
        hn = y.astype(BF16)
        hn_ref[...] = hn
        a = _dot(hn, w1_ref[...])
        a_ref[...] = a
        r = jnp.square(jnp.maximum(a, 0.0)).astype(BF16)
        r_ref[...] = r
        o_ref[...] = x + _dot(r, w2_ref[...])

    row = lambda w: pl.BlockSpec((tm, w), lambda i: (i, 0))
    vmem = pl.BlockSpec(memory_space=pltpu.VMEM)
    return pl.pallas_call(
        body, name=name, grid=(lp // tm,),
        in_specs=[row(D_MODEL), pl.BlockSpec(g.shape, lambda i: (0, 0)), vmem, vmem],
        out_specs=[row(D_MODEL), row(D_FF), row(D_FF), row(D_MODEL)],
        out_shape=[SDS((lp, D_MODEL), BF16), SDS((lp, D_FF), F32), SDS((lp, D_FF), BF16), SDS((lp, D_MODEL), F32)],
        compiler_params=_cparams(("parallel",), 60),
    )(h, g, w1, w2)


def _mlp_bwd(dh2, h1, g, a, w1, w2, name):
    lp = h1.shape[0]
    tm = _row_tile(lp) // MLP_TILE_DIV

    def body(dy_ref, h_ref, g_ref, a_ref, w1_ref, w2_ref, da_ref, dh_ref, dg_ref):
        @pl.when(pl.program_id(0) == 0)
        def _():
            dg_ref[...] = jnp.zeros_like(dg_ref)
        dy = dy_ref[...]
        dr = _dot_nt(dy.astype(BF16), w2_ref[...])
        da = (dr * (2.0 * jnp.maximum(a_ref[...], 0.0))).astype(BF16)
        da_ref[...] = da
        dhn = _dot_nt(da, w1_ref[...])
        r, xn, _ = _rms_fwd(h_ref[...], g_ref[...])
        dg_ref[...] += _colsum(dhn * xn)
        dh_ref[...] = dy + _rms_bwd(r, xn, g_ref[...], dhn)

    row = lambda w: pl.BlockSpec((tm, w), lambda i: (i, 0))
    vmem = pl.BlockSpec(memory_space=pltpu.VMEM)
    vec = pl.BlockSpec((1, D_MODEL), lambda i: (0, 0))
    return pl.pallas_call(
        body, name=name, grid=(lp // tm,),
        in_specs=[row(D_MODEL), row(D_MODEL), vec, row(D_FF), vmem, vmem],
        out_specs=[row(D_FF), row(D_MODEL), vec],
        out_shape=[SDS((lp, D_FF), BF16), SDS((lp, D_MODEL), F32), SDS((1, D_MODEL), F32)],
        compiler_params=_cparams(("arbitrary",), 60),
    )(dh2, h1, g, a, w1, w2)


def _mix_bwd(dh1, wa, wc, attn, seg, name):
    lp = dh1.shape[0]
    tm = _row_tile(lp)

    def body(dy_ref, wa_ref, wc_ref, o_ref, seg_ref, do_ref, dcs_ref, dl_ref):
        dy = dy_ref[...].astype(BF16)
        do = _dot_nt(dy, wa_ref[...])
        do_ref[...] = do.astype(BF16)
        dcs_ref[...] = _dot_nt(dy, wc_ref[...])
        dl_ref[...] = jnp.dot(do * o_ref[...].astype(F32), seg_ref[...], precision=HIGHEST,
                              preferred_element_type=F32)

    row = lambda w: pl.BlockSpec((tm, w), lambda i: (i, 0))
    full = lambda x: pl.BlockSpec(x.shape, lambda i: (0, 0))
    return pl.pallas_call(
        body, name=name, grid=(lp // tm,),
        in_specs=[row(D_MODEL), full(wa), full(wc), row(D_ATTN), full(seg)],
        out_specs=[row(D_ATTN), row(D_ATTN), row(D_ATTN)],
        out_shape=[SDS((lp, D_ATTN), BF16), SDS((lp, D_ATTN), F32), SDS((lp, D_ATTN), F32)],
        compiler_params=_cparams(("parallel",), 48),
    )(dh1, wa, wc, attn, seg)


def _in_bwd(dh1, h, g, dq, dk, dv, drest, wqkv, wrest, name):
    lp = h.shape[0]
    tm = _row_tile(lp)

    def body(dy_ref, h_ref, g_ref, dq_ref, dk_ref, dv_ref, dr_ref, wq_ref, wr_ref, dh_ref, dg_ref):
        @pl.when(pl.program_id(0) == 0)
        def _():
            dg_ref[...] = jnp.zeros_like(dg_ref)
        dhn = (_dot_nt(dq_ref[...], wq_ref[:, 0:D_ATTN]) + _dot_nt(dk_ref[...], wq_ref[:, D_ATTN:2 * D_ATTN])
               + _dot_nt(dv_ref[...], wq_ref[:, 2 * D_ATTN:]) + _dot_nt(dr_ref[...], wr_ref[...]))
        r, xn, _ = _rms_fwd(h_ref[...], g_ref[...])
        dg_ref[...] += _colsum(dhn * xn)
        dh_ref[...] = dy_ref[...] + _rms_bwd(r, xn, g_ref[...], dhn)

    row = lambda w: pl.BlockSpec((tm, w), lambda i: (i, 0))
    full = lambda x: pl.BlockSpec(x.shape, lambda i: (0, 0))
    vec = pl.BlockSpec((1, D_MODEL), lambda i: (0, 0))
    return pl.pallas_call(
        body, name=name, grid=(lp // tm,),
        in_specs=[row(D_MODEL), row(D_MODEL), vec, row(D_ATTN), row(D_ATTN), row(D_ATTN), row(N_REST),
                  full(wqkv), full(wrest)],
        out_specs=[row(D_MODEL), vec],
        out_shape=[SDS((lp, D_MODEL), F32), SDS((1, D_MODEL), F32)],
        compiler_params=_cparams(("arbitrary",), 48),
    )(dh1, h, g, dq, dk, dv, drest, wqkv, wrest)


def _loss_head(h, target, g, seq, name):
    lp = h.shape[0]
    tm = _row_tile(lp)

    def body(h_ref, t_ref, g_ref, dh_ref, loss_ref, dg_ref):
        i = pl.program_id(0)

        @pl.when(i == 0)
        def _():
            dg_ref[...] = jnp.zeros_like(dg_ref)
            loss_ref[...] = jnp.zeros_like(loss_ref)
        r, xn, y = _rms_fwd(h_ref[...], g_ref[...])
        rows = i * tm + lax.broadcasted_iota(jnp.int32, (tm, 1), 0)
        real = (rows >= N_META) & (rows < N_META + seq)
        e = jnp.where(real, y - t_ref[...], 0.0)
        loss_ref[...] += 0.5 * jnp.sum(jnp.mean(e * e, axis=-1, keepdims=True), axis=0, keepdims=True)
        dy = e * (1.0 / D_MODEL)
        dg_ref[...] += _colsum(dy * xn)
        dh_ref[...] = _rms_bwd(r, xn, g_ref[...], dy)

    row = pl.BlockSpec((tm, D_MODEL), lambda i: (i, 0))
    vec = pl.BlockSpec((1, D_MODEL), lambda i: (0, 0))
    return pl.pallas_call(
        body, name=name, grid=(lp // tm,),
        in_specs=[row, row, vec],
        out_specs=[row, pl.BlockSpec((1, 1), lambda i: (0, 0)), vec],
        out_shape=[SDS((lp, D_MODEL), F32), SDS((1, 1), F32), SDS((1, D_MODEL), F32)],
        compiler_params=_cparams(("arbitrary",), 48),
    )(h, target, g)


def _shift_down(xcat, j):
    return (xcat if j == 0 else pltpu.roll(xcat, j, 0))[HALO:, :]


def _shift_up(xcat, j, tm):
    n = xcat.shape[0]
    return (xcat if j == 0 else pltpu.roll(xcat, n - j, 0))[:tm, :]


def _layernorm_parts(x):
    mu = jnp.mean(x, axis=-1, keepdims=True)
    xc = x - mu
    rstd = lax.rsqrt(jnp.mean(xc * xc, axis=-1, keepdims=True) + EPS)
    return rstd, xc * rstd


def _log_sigmoid(z):
    return jnp.minimum(z, 0.0) - jnp.log(1.0 + jnp.exp(-jnp.abs(z)))


def _conv_fwd(rest, wdw, vec, wpw, wsc, name):
    lp = rest.shape[0]
    tm = _row_tile(lp)
    per = tm // HALO

    def body(a_ref, g_ref, ah_ref, gh_ref, b_ref, c_ref, u_ref, ch_ref, uh_ref, f_ref,
             wdw_ref, vec_ref, wpw_ref, wsc_ref, cs_ref, dwc_ref, cv_ref, cc_ref, carry):
        i = pl.program_id(0)
        first = i == 0

        @pl.when(first)
        def _():
            carry[...] = jnp.zeros_like(carry)
        glu = a_ref[...] * _sigmoid(g_ref[...])
        glu_h = jnp.where(first, 0.0, ah_ref[...] * _sigmoid(gh_ref[...]))
        xcat = jnp.concatenate([glu_h, glu], axis=0)
        acc = jnp.zeros((tm, D_CONF), F32) + vec_ref[0:1, :]
        for k in range(CONF_K):
            acc = acc + wdw_ref[k:k + 1, :] * _shift_down(xcat, CONF_K - 1 - k)
        dwc_ref[...] = acc
        _, y = _layernorm_parts(acc)
        ln = y * vec_ref[1:2, :] + vec_ref[2:3, :]
        sw = ln * _sigmoid(ln)
        conf = _dot(sw.astype(BF16), wpw_ref[...]) + vec_ref[3:4, :]
        p = c_ref[...] * u_ref[...]
        p_h = jnp.where(first, 0.0, ch_ref[...] * uh_ref[...])
        pcat = jnp.concatenate([p_h, p], axis=0)
        cv = jnp.zeros((tm, D_CONF), F32)
        for k in range(SC_K):
            cv = cv + wsc_ref[k:k + 1, :] * _shift_down(pcat, SC_K - 1 - k)
        cv_ref[...] = cv
        cs_ref[:, :D_CONF] = conf.astype(BF16)
        cs_ref[:, D_CONF:] = (b_ref[...] * cv).astype(BF16)
        logf = _log_sigmoid(f_ref[...] + vec_ref[4:5, :N_F_PAD])
        tri = (lax.broadcasted_iota(jnp.int32, (tm, tm), 0) >= lax.broadcasted_iota(jnp.int32, (tm, tm), 1))
        c = jnp.dot(tri.astype(F32), logf, precision=HIGHEST, preferred_element_type=F32) + carry[0:1, :]
        cc_ref[...] = c
        carry[0:1, :] = c[tm - 1:tm, :]

    cur = lambda j: pl.BlockSpec((tm, D_CONF), lambda i: (i, j))
    prev = lambda j: pl.BlockSpec((HALO, D_CONF), lambda i: (jnp.maximum(i * per - 1, 0), j))
    full = lambda x: pl.BlockSpec(x.shape, lambda i: (0, 0))
    return pl.pallas_call(
        body, name=name, grid=(lp // tm,),
        in_specs=[cur(0), cur(1), prev(0), prev(1), cur(2), cur(3), cur(4), prev(3), prev(4),
                  pl.BlockSpec((tm, N_F_PAD), lambda i: (i, F_BLK)),
                  full(wdw), full(vec), full(wpw), full(wsc)],
        out_specs=[pl.BlockSpec((tm, 2 * D_CONF), lambda i: (i, 0)), cur(0), cur(0),
                   pl.BlockSpec((tm, N_F_PAD), lambda i: (i, 0))],
        out_shape=[SDS((lp, 2 * D_CONF), BF16), SDS((lp, D_CONF), F32), SDS((lp, D_CONF), F32),
                   SDS((lp, N_F_PAD), F32)],
        scratch_shapes=[pltpu.VMEM((8, N_F_PAD), F32)],
        compiler_params=_cparams(("arbitrary",), 48),
    )(rest, rest, rest, rest, rest, rest, rest, rest, rest, rest, wdw, vec, wpw, wsc)


def _conv_bwd(dcs, dwc, cv, rest, dccol, wdw, vec, wpw, wsc, name):
    lp = rest.shape[0]
    tm = _row_tile(lp)
    per = tm // HALO
    nt = lp // tm
    n_halo_blocks = lp // HALO

    def body(dcf_ref, dsc_ref, dcfn_ref, dscn_ref, dwc_ref, dwcn_ref, cv_ref,
             a_ref, g_ref, ah_ref, gh_ref, b_ref, bn_ref, c_ref, u_ref, ch_ref, uh_ref, f_ref, dcc_ref,
             wdw_ref, vec_ref, wpw_ref, wsc_ref,
             dr_ref, dwdw_ref, dwsc_ref, dwpw_ref, dvec_ref, carry):
        i = pl.program_id(0)
        first = i == nt - 1
        last = i == 0

        @pl.when(i == 0)
        def _():
            carry[...] = jnp.zeros_like(carry)
            dwdw_ref[...] = jnp.zeros_like(dwdw_ref)
            dwsc_ref[...] = jnp.zeros_like(dwsc_ref)
            dwpw_ref[...] = jnp.zeros_like(dwpw_ref)
            dvec_ref[...] = jnp.zeros_like(dvec_ref)

        ln_g, ln_b = vec_ref[1:2, :], vec_ref[2:3, :]

        def ln_bwd(dconf, x):
            rstd, y = _layernorm_parts(x)
            ln = y * ln_g + ln_b
            sg = _sigmoid(ln)
            dsw = _dot_nt(dconf.astype(BF16), wpw_ref[...])
            dln = dsw * (sg * (1.0 + ln * (1.0 - sg)))
            dy = dln * ln_g
            ddw = rstd * (dy - jnp.mean(dy, axis=-1, keepdims=True) - y * jnp.mean(dy * y, axis=-1, keepdims=True))
            return ln * sg, y, dln, ddw

        dconf = dcf_ref[...]
        sw, y, dln, ddw = ln_bwd(dconf, dwc_ref[...])
        _, _, _, ddw_n = ln_bwd(dcfn_ref[...], dwcn_ref[...])
        ddw_n = jnp.where(last, 0.0, ddw_n)
        dvec_ref[0:1, :] += _colsum(ddw)
        dvec_ref[1:2, :] += _colsum(dln * y)
        dvec_ref[2:3, :] += _colsum(dln)
        dvec_ref[3:4, :] += _colsum(dconf)
        dwpw_ref[...] += _dot_tn(sw.astype(BF16), dconf.astype(BF16))

        a, sg = a_ref[...], _sigmoid(g_ref[...])
        glu = a * sg
        glu_h = jnp.where(first, 0.0, ah_ref[...] * _sigmoid(gh_ref[...]))
        xcat = jnp.concatenate([glu_h, glu], axis=0)
        dcat = jnp.concatenate([ddw, ddw_n], axis=0)
        dglu = jnp.zeros((tm, D_CONF), F32)
        for k in range(CONF_K):
            j = CONF_K - 1 - k
            dwdw_ref[k:k + 1, :] += _colsum(ddw * _shift_down(xcat, j))
            dglu = dglu + wdw_ref[k:k + 1, :] * _shift_up(dcat, j, tm)
        dr_ref[:, 0:D_CONF] = (dglu * sg).astype(BF16)
        dr_ref[:, D_CONF:2 * D_CONF] = (dglu * a * sg * (1.0 - sg)).astype(BF16)

        dsc = dsc_ref[...]
        b, c, u = b_ref[...], c_ref[...], u_ref[...]
        dcv = dsc * b
        dcv_n = jnp.where(last, 0.0, dscn_ref[...] * bn_ref[...])
        p_h = jnp.where(first, 0.0, ch_ref[...] * uh_ref[...])
        pcat = jnp.concatenate([p_h, c * u], axis=0)
        dccat = jnp.concatenate([dcv, dcv_n], axis=0)
        dp = jnp.zeros((tm, D_CONF), F32)
        for k in range(SC_K):
            j = SC_K - 1 - k
            dwsc_ref[k:k + 1, :] += _colsum(dcv * _shift_down(pcat, j))
            dp = dp + wsc_ref[k:k + 1, :] * _shift_up(dccat, j, tm)
        dr_ref[:, 2 * D_CONF:3 * D_CONF] = (dsc * cv_ref[...]).astype(BF16)
        dr_ref[:, 3 * D_CONF:4 * D_CONF] = (dp * u).astype(BF16)
        dr_ref[:, 4 * D_CONF:5 * D_CONF] = (dp * c).astype(BF16)

        tri = (lax.broadcasted_iota(jnp.int32, (tm, tm), 0) <= lax.broadcasted_iota(jnp.int32, (tm, tm), 1))
        dlogf = jnp.dot(tri.astype(F32), dcc_ref[...], precision=HIGHEST, preferred_element_type=F32) + carry[0:1, :]
        carry[0:1, :] = dlogf[0:1, :]
        dz = dlogf * _sigmoid(-(f_ref[...] + vec_ref[4:5, :N_F_PAD]))
        dvec_ref[4:5, 0:N_F_PAD] += _colsum(dz)
        dr_ref[:, 5 * D_CONF:] = dz.astype(BF16)

    rt = lambda i: nt - 1 - i
    cur = lambda j: pl.BlockSpec((tm, D_CONF), lambda i: (rt(i), j))
    prev = lambda j: pl.BlockSpec((HALO, D_CONF), lambda i: (jnp.maximum(rt(i) * per - 1, 0), j))
    nxt = lambda j: pl.BlockSpec((HALO, D_CONF), lambda i: (jnp.minimum((rt(i) + 1) * per, n_halo_blocks - 1), j))
    full = lambda x: pl.BlockSpec(x.shape, lambda i: (0, 0))
    acc = lambda r, w: pl.BlockSpec((r, w), lambda i: (0, 0))
    return pl.pallas_call(
        body, name=name, grid=(nt,),
        in_specs=[cur(0), cur(1), nxt(0), nxt(1), cur(0), nxt(0), cur(0),
                  cur(0), cur(1), prev(0), prev(1), cur(2), nxt(2), cur(3), cur(4), prev(3), prev(4),
                  pl.BlockSpec((tm, N_F_PAD), lambda i: (rt(i), F_BLK)),
                  pl.BlockSpec((tm, N_F_PAD), lambda i: (rt(i), 0)),
                  full(wdw), full(vec), full(wpw), full(wsc)],
        out_specs=[pl.BlockSpec((tm, N_REST), lambda i: (rt(i), 0)), acc(32, D_CONF), acc(8, D_CONF),
                   acc(D_CONF, D_CONF), acc(8, D_CONF)],
        out_shape=[SDS((lp, N_REST), BF16), SDS((32, D_CONF), F32), SDS((8, D_CONF), F32),
                   SDS((D_CONF, D_CONF), F32), SDS((8, D_CONF), F32)],
        scratch_shapes=[pltpu.VMEM((8, N_F_PAD), F32)],
        compiler_params=_cparams(("arbitrary",), 48),
    )(dcs, dcs, dcs, dcs, dwc, dwc, cv, rest, rest, rest, rest, rest, rest, rest, rest, rest, rest, rest, dccol,
      wdw, vec, wpw, wsc)


def _causal_tiles(n, key_major):
    if key_major:
        pairs = [(q, k) for k in range(n) for q in range(k, n)]
    else:
        pairs = [(q, k) for q in range(n) for k in range(q + 1)]
    return (jnp.asarray([p[0] for p in pairs], jnp.int32), jnp.asarray([p[1] for p in pairs], jnp.int32))


def _attn_fwd(qkv, crow, cpair, name):
    lp = qkv.shape[0]
    t = _row_tile(lp)
    nq = lp // t

    qtab, ktab = _causal_tiles(nq, key_major=False)

    def body(qt_ref, kt_ref, q_ref, k_ref, v_ref, c_ref, cq_ref, o_ref, lse_ref, m_scr, l_scr, acc_scr):
        qi, ki = qt_ref[pl.program_id(1)], kt_ref[pl.program_id(1)]

        @pl.when(ki == 0)
        def _():
            lane = lax.broadcasted_iota(jnp.int32, (t, PAIR_W), 1)
            m_scr[...] = jnp.full_like(m_scr, NEG)
            l_scr[...] = jnp.where(lane < 2, 0.0, 1.0)
            acc_scr[...] = jnp.zeros_like(acc_scr)

        def step(diagonal):
            q, k, v = q_ref[...], k_ref[...], v_ref[...]
            lane = lax.broadcasted_iota(jnp.int32, (t, PAIR_W), 1)
            lo = lane < HEAD_DIM
            m_all, l_all = m_scr[...], l_scr[...]
            pv, alpha, m_out, l_out = [], [], [], []
            for j in range(2):
                sel = lo if j == 0 else jnp.logical_not(lo)
                s = _dot_nt(jnp.where(sel, q, jnp.zeros_like(q)), k) + (cq_ref[:, j:j + 1] - c_ref[j:j + 1, :])
                if diagonal:
                    causal = (lax.broadcasted_iota(jnp.int32, (t, t), 0) >= lax.broadcasted_iota(jnp.int32, (t, t), 1))
                    s = jnp.where(causal, s, -jnp.inf)
                m_prev = m_all[:, j:j + 1]
                m_new = jnp.maximum(m_prev, jnp.max(s, axis=1, keepdims=True))
                al = jnp.exp(m_prev - m_new)
                p = jnp.exp(s - m_new)
                l_out.append(al * l_all[:, j:j + 1] + jnp.sum(p, axis=1, keepdims=True))
                m_out.append(m_new)
                pv.append(_dot(p.astype(BF16), v))
                alpha.append(al)
            m_scr[...] = jnp.where(lane == 0, m_out[0], jnp.where(lane == 1, m_out[1], m_all))
            l_scr[...] = jnp.where(lane == 0, l_out[0], jnp.where(lane == 1, l_out[1], l_all))
            acc_scr[...] = acc_scr[...] * jnp.where(lo, alpha[0], alpha[1]) + jnp.where(lo, pv[0], pv[1])

        @pl.when(ki < qi)
        def _():
            step(False)

        @pl.when(ki == qi)
        def _():
            step(True)
            lo = lax.broadcasted_iota(jnp.int32, (t, PAIR_W), 1) < HEAD_DIM
            inv = jnp.where(lo, 1.0 / l_scr[:, 0:1], 1.0 / l_scr[:, 1:2])
            o_ref[...] = (acc_scr[...] * inv).astype(BF16)
            lse_ref[...] = m_scr[...] + jnp.log(l_scr[...])

    kv = lambda off: pl.BlockSpec((t, PAIR_W), lambda p, s, qt, kt: (kt[s], off + p))
    qblk = pl.BlockSpec((t, PAIR_W), lambda p, s, qt, kt: (qt[s], p))
    return pl.pallas_call(
        body, name=name,
        grid_spec=pltpu.PrefetchScalarGridSpec(
            num_scalar_prefetch=2, grid=(N_PAIRS, qtab.shape[0]),
            in_specs=[qblk, kv(N_PAIRS), kv(2 * N_PAIRS),
                      pl.BlockSpec((None, 2, t), lambda p, s, qt, kt: (p, 0, kt[s])), qblk],
            out_specs=[qblk, qblk],
            scratch_shapes=[pltpu.VMEM((t, PAIR_W), F32), pltpu.VMEM((t, PAIR_W), F32), pltpu.VMEM((t, PAIR_W), F32)]),
        out_shape=[SDS((lp, D_ATTN), BF16), SDS((lp, D_ATTN), F32)],
        compiler_params=_cparams(("parallel", "arbitrary"), 48),
    )(qtab, ktab, qkv, qkv, qkv, crow, cpair)


def _attn_bwd(qkv, do, crow, cpair, lse, delta, name):
    lp = qkv.shape[0]
    t = _row_tile(lp)
    nq = lp // t
    qtab, ktab = _causal_tiles(nq, key_major=True)

    def body(qt_ref, kt_ref, q_ref, k_ref, v_ref, do_ref, c_ref, cq_ref, lse_ref, dl_ref,
             dq_ref, dk_ref, dv_ref, dck_ref, dcq_ref, dq_scr, dcq_scr, dk_scr, dv_scr, dck_scr):
        step_id = pl.program_id(1)
        qi, ki = qt_ref[step_id], kt_ref[step_id]

        @pl.when(step_id == 0)
        def _():
            dq_scr[...] = jnp.zeros_like(dq_scr)
            dcq_scr[...] = jnp.zeros_like(dcq_scr)

        @pl.when(qi == ki)
        def _():
            dk_scr[...] = jnp.zeros_like(dk_scr)
            dv_scr[...] = jnp.zeros_like(dv_scr)
            dck_scr[...] = jnp.zeros_like(dck_scr)

        def step(diagonal):
            q, k, v, dout = q_ref[...], k_ref[...], v_ref[...], do_ref[...]
            lane = lax.broadcasted_iota(jnp.int32, (t, PAIR_W), 1)
            lo = lane < HEAD_DIM
            zero, one = jnp.zeros_like(q), jnp.ones_like(q)
            dq_new = jnp.zeros((t, PAIR_W), F32)
            dv_new = jnp.zeros((t, PAIR_W), F32)
            dk_new = jnp.zeros((t, PAIR_W), F32)
            col_sums, row_sums = [], []
            for j in range(2):
                sel = lo if j == 0 else jnp.logical_not(lo)
                qj, kj, doj = jnp.where(sel, q, zero), jnp.where(sel, k, zero), jnp.where(sel, dout, zero)
                p = jnp.exp(_dot_nt(qj, k) + (cq_ref[:, j:j + 1] - c_ref[j:j + 1, :]) - lse_ref[:, j:j + 1])
                if diagonal:
                    causal = (lax.broadcasted_iota(jnp.int32, (t, t), 0) >= lax.broadcasted_iota(jnp.int32, (t, t), 1))
                    p = jnp.where(causal, p, 0.0)
                dp = _dot_nt(doj, v)
                dsb = (p * (dp - dl_ref[:, j:j + 1])).astype(BF16)
                dv_new = dv_new + _dot_tn(p.astype(BF16), doj)
                ones_lane = HEAD_DIM * (1 - j)
                dk_j = _dot_tn(dsb, jnp.where(lane == ones_lane, one, qj))
                dq_j = _dot(dsb, jnp.where(lane == ones_lane, one, kj))
                dk_new = dk_new + jnp.where(sel, dk_j, 0.0)
                dq_new = dq_new + jnp.where(sel, dq_j, 0.0)
                col_sums.append(dk_j[:, ones_lane:ones_lane + 1])
                row_sums.append(dq_j[:, ones_lane:ones_lane + 1])
            dv_scr[...] += dv_new
            dk_scr[...] += dk_new
            dck_scr[...] += jnp.where(lane == 0, col_sums[0], jnp.where(lane == 1, col_sums[1], 0.0))
            rows = pl.ds(pl.multiple_of(qi * t, t), t)
            dq_scr[rows, :] += dq_new
            dcq_scr[rows, :] += jnp.where(lane == 0, row_sums[0], jnp.where(lane == 1, row_sums[1], 0.0))

        @pl.when(qi > ki)
        def _():
            step(False)

        @pl.when(qi == ki)
        def _():
            step(True)
            rows = pl.ds(pl.multiple_of(qi * t, t), t)
            dq_ref[...] = (dq_scr[rows, :] * (HEAD_DIM ** -0.5)).astype(BF16)
            dcq_ref[...] = dcq_scr[rows, :]

        @pl.when(qi == nq - 1)
        def _():
            dk_ref[...] = dk_scr[...].astype(BF16)
            dv_ref[...] = dv_scr[...].astype(BF16)
            dck_ref[...] = dck_scr[...]

    qside = lambda off: pl.BlockSpec((t, PAIR_W), lambda p, s, qt, kt: (qt[s], off + p))
    kside = lambda off: pl.BlockSpec((t, PAIR_W), lambda p, s, qt, kt: (kt[s], off + p))
    crow_blk = pl.BlockSpec((None, 2, t), lambda p, s, qt, kt: (p, 0, kt[s]))
    return pl.pallas_call(
        body, name=name,
        grid_spec=pltpu.PrefetchScalarGridSpec(
            num_scalar_prefetch=2, grid=(N_PAIRS, qtab.shape[0]),
            in_specs=[qside(0), kside(N_PAIRS), kside(2 * N_PAIRS), qside(0), crow_blk, qside(0), qside(0), qside(0)],
            out_specs=[kside(0), kside(0), kside(0), kside(0), kside(0)],
            scratch_shapes=[pltpu.VMEM((lp, PAIR_W), F32), pltpu.VMEM((lp, PAIR_W), F32), pltpu.VMEM((t, PAIR_W), F32),
                            pltpu.VMEM((t, PAIR_W), F32), pltpu.VMEM((t, PAIR_W), F32)]),
        out_shape=[SDS((lp, D_ATTN), BF16), SDS((lp, D_ATTN), BF16), SDS((lp, D_ATTN), BF16),
                   SDS((lp, D_ATTN), F32), SDS((lp, D_ATTN), F32)],
        compiler_params=_cparams(("parallel", "arbitrary"), 48),
    )(qtab, ktab, qkv, qkv, qkv, do, crow, cpair, lse, delta)


def _shard_sum(own, parts, name, slot_order=False):
    rows, cols = own.shape
    tr = rows if rows <= 512 else 512
    assert rows % tr == 0

    def body(me_ref, own_ref, parts_ref, o_ref):
        if slot_order:
            acc = parts_ref[0].astype(F32)
            for j in range(1, 4):
                acc = acc + parts_ref[j].astype(F32)
        else:
            me = me_ref[0]
            acc = own_ref[...]
            for j in range(4):
                acc = acc + jnp.where(me == j, 0.0, parts_ref[j].astype(F32))
        o_ref[...] = acc

    me = (2 * lax.axis_index("x") + lax.axis_index("y")).astype(jnp.int32).reshape(1)
    return pl.pallas_call(
        body, name=name,
        grid_spec=pltpu.PrefetchScalarGridSpec(
            num_scalar_prefetch=1, grid=(rows // tr,),
            in_specs=[pl.BlockSpec((tr, cols), lambda i, me: (i, 0)), pl.BlockSpec((4, tr, cols), lambda i, me: (0, i, 0))],
            out_specs=pl.BlockSpec((tr, cols), lambda i, me: (i, 0))),
        out_shape=SDS((rows, cols), F32),
        compiler_params=_cparams(("parallel",), 48),
    )(me, own, parts)


def _adamw(w, m, v, s_own, s_sib, name):
    rows, cols = w.shape
    tr = rows if rows <= 256 else 256
    assert rows % tr == 0

    def body(w_ref, m_ref, v_ref, a_ref, b_ref, g_ref, d_ref, nm_ref, nv_ref):
        g = a_ref[...] + b_ref[...]
        g_ref[...] = g
        m = ADAM_B1 * m_ref[...] + (1.0 - ADAM_B1) * g
        v = ADAM_B2 * v_ref[...] + (1.0 - ADAM_B2) * jnp.square(g)
        nm_ref[...] = m
        nv_ref[...] = v
        m_hat = m / (1.0 - ADAM_B1 ** ADAM_STEP)
        v_hat = v / (1.0 - ADAM_B2 ** ADAM_STEP)
        d_ref[...] = -ADAM_LR * (m_hat / (jnp.sqrt(v_hat) + ADAM_EPS) + ADAM_WD * w_ref[...])

    blk = pl.BlockSpec((tr, cols), lambda i: (i, 0))
    return pl.pallas_call(
        body, name=name, grid=(rows // tr,), in_specs=[blk] * 5, out_specs=[blk] * 4,
        out_shape=[SDS((rows, cols), F32)] * 4,
        compiler_params=_cparams(("parallel",), 48),
    )(w, m, v, s_own, s_sib)


def _pad_rows(a, rows):
    return jnp.pad(a, ((0, rows - a.shape[0]), (0, 0)))


def _flat_rows(arrs, cols, rows):
    flat = jnp.concatenate([a.reshape(-1) for a in arrs])
    return jnp.pad(flat, (0, rows * cols - flat.shape[0])).reshape(rows, cols)


def _unflat(buf, shapes):
    flat, out, off = buf.reshape(-1), [], 0
    for s in shapes:
        n = 1
        for d in s:
            n *= d
        out.append(flat[off:off + n].reshape(s))
        off += n
    return out


def kernel(x, meta_tokens, mix_norm_g, w_in, b_forget, w_conf_dw, b_conf_dw, conf_ln_g, conf_ln_b, w_conf_pw, b_conf_pw, w_sc_conv, w_out, mlp_norm_g, w_mlp1, w_mlp2, final_norm_g, loss_target, m_meta_tokens, m_mix_norm_g, m_w_in, m_b_forget, m_w_conf_dw, m_b_conf_dw, m_conf_ln_g, m_conf_ln_b, m_w_conf_pw, m_b_conf_pw, m_w_sc_conv, m_w_out, m_mlp_norm_g, m_w_mlp1, m_w_mlp2, m_final_norm_g, v_meta_tokens, v_mix_norm_g, v_w_in, v_b_forget, v_w_conf_dw, v_b_conf_dw, v_conf_ln_g, v_conf_ln_b, v_w_conf_pw, v_b_conf_pw, v_w_sc_conv, v_w_out, v_mlp_norm_g, v_w_mlp1, v_w_mlp2, v_final_norm_g):
    depth = w_in.shape[0]
    seq = x.shape[1]
    l_tok = N_META + seq
    lp = _padded_len(l_tok)
    n_in = w_in.shape[2] * 4
    n_in_sh = w_in.shape[2]
    me = 2 * lax.axis_index("x") + lax.axis_index("y")

    small_shapes = [w_conf_dw.shape, w_conf_pw.shape, w_sc_conv.shape, meta_tokens.shape]
    n_small = sum(a.size for a in (w_conf_dw, w_conf_pw, w_sc_conv, meta_tokens))
    small_rows = -(-n_small // 1024 // 8) * 8
    small = _flat_rows([w_conf_dw, w_conf_pw, w_sc_conv, meta_tokens], 1024, small_rows)
    g_in, g_out, g_m1, g_m2, g_small = _weights_all_gather(
        [w_in.astype(BF16).reshape(depth * D_MODEL, n_in_sh), w_out.astype(BF16).reshape(-1, D_MODEL),
         w_mlp1.astype(BF16).reshape(depth * D_MODEL, -1), w_mlp2.astype(BF16).reshape(-1, D_MODEL), small],
        name="weights_all_gather")
    w_in_f = jnp.concatenate([g_in[j].reshape(depth, D_MODEL, n_in_sh) for j in range(4)], axis=2)
    w_out_f = jnp.concatenate([g_out[j].reshape(depth, -1, D_MODEL) for j in range(4)], axis=1)
    w_m1_f = jnp.concatenate([g_m1[j].reshape(depth, D_MODEL, -1) for j in range(4)], axis=2)
    w_m2_f = jnp.concatenate([g_m2[j].reshape(depth, -1, D_MODEL) for j in range(4)], axis=1)
    sm = [_unflat(g_small[j], small_shapes) for j in range(4)]
    w_dw_f = jnp.concatenate([s[0] for s in sm], axis=2)
    w_pw_f = jnp.concatenate([s[1] for s in sm], axis=1)
    w_sc_f = jnp.concatenate([s[2] for s in sm], axis=2)
    meta_f = jnp.concatenate([s[3] for s in sm], axis=1)

    q_end, f_end = 3 * D_ATTN, 3 * D_ATTN + N_HEADS
    wqkv = [w_in_f[l, :, :q_end] for l in range(depth)]
    wrest = [jnp.concatenate([w_in_f[l, :, f_end:], w_in_f[l, :, q_end:f_end],
                              jnp.zeros((D_MODEL, N_F_PAD - N_HEADS), BF16)], axis=1) for l in range(depth)]
    wdw = [_pad_rows(w_dw_f[l], 32) for l in range(depth)]
    wsc = [_pad_rows(w_sc_f[l], 8) for l in range(depth)]
    wpw = [w_pw_f[l].astype(BF16) for l in range(depth)]
    vec = [_pad_rows(jnp.stack([b_conf_dw[l], conf_ln_g[l], conf_ln_b[l], b_conf_pw[l],
                                jnp.pad(b_forget[l], (0, D_CONF - N_HEADS))]), 8) for l in range(depth)]
    col = jnp.arange(D_ATTN)
    seg = (col[None, :] == ((col // PAIR_W) * PAIR_W + (col % PAIR_W) // HEAD_DIM)[:, None]).astype(F32)

    h = jnp.concatenate([meta_f, x[0], jnp.zeros((lp - l_tok, D_MODEL), F32)], axis=0)
    target = jnp.pad(loss_target[0], ((N_META, lp - l_tok), (0, 0)))
    saved = []
    for l in range(depth):
        g1, g2 = mix_norm_g[l][None, :], mlp_norm_g[l][None, :]
        hn1, qkv, rest = _proj_fwd(h, g1, wqkv[l], wrest[l], name=f"proj_fwd_{l}")
        cs, dwc, cv, ccol = _conv_fwd(rest, wdw[l], vec[l], wpw[l], wsc[l], name=f"conv_fwd_{l}")
        crow = ccol[:, :N_HEADS].T.reshape(N_PAIRS, 2, lp)
        cpair = jnp.pad(ccol[:, :N_HEADS].reshape(lp, N_PAIRS, 2), ((0, 0), (0, 0), (0, PAIR_W - 2))).reshape(lp, D_ATTN)
        attn, lse = _attn_fwd(qkv, crow, cpair, name=f"attn_fwd_{l}")
        h1 = _mix_out(h, attn, cs, w_out_f[l, :D_ATTN], w_out_f[l, D_ATTN:], name=f"mix_out_{l}")
        hn2, a, r, h2 = _mlp_fwd(h1, g2, w_m1_f[l], w_m2_f[l], name=f"mlp_fwd_{l}")
        saved.append((h, g1, g2, hn1, qkv, rest, cs, dwc, cv, crow, cpair, attn, lse, h1, hn2, a, r))
        h = h2
    dh, loss11, d_final_g = _loss_head(h, target, final_norm_g[None, :], seq, name="loss_head")
    loss = lax.psum(loss11[0, 0], ("x", "y", "c"))

    grads = [None] * depth
    for l in reversed(range(depth)):
        h0, g1, g2, hn1, qkv, rest, cs, dwc, cv, crow, cpair, attn, lse, h1, hn2, a, r = saved[l]
        da, dh1, dg2 = _mlp_bwd(dh, h1, g2, a, w_m1_f[l], w_m2_f[l], name=f"mlp_bwd_{l}")
        d_w2 = _mm_tn(r, dh, name=f"dw_mlp2_{l}")
        d_w1 = _mm_tn(hn2, da, name=f"dw_mlp1_{l}")
        do, dcs, delta = _mix_bwd(dh1, w_out_f[l, :D_ATTN], w_out_f[l, D_ATTN:], attn, seg, name=f"mix_bwd_{l}")
        d_wout = jnp.concatenate([_mm_tn(attn, dh1, name=f"dw_out_a_{l}"), _mm_tn(cs, dh1, name=f"dw_out_c_{l}")], axis=0)
        dq, dk, dv, dck, dcq = _attn_bwd(qkv, do, crow, cpair, lse, delta, name=f"attn_bwd_{l}")
        dc_heads = (dcq - dck).reshape(lp, N_PAIRS, PAIR_W)[:, :, :2].reshape(lp, N_HEADS)
        dccol = jnp.pad(dc_heads, ((0, 0), (0, N_F_PAD - N_HEADS)))
        drest, d_wdw, d_wsc, d_wpw, d_vec = _conv_bwd(dcs, dwc, cv, rest, dccol, wdw[l], vec[l], wpw[l], wsc[l],
                                                      name=f"conv_bwd_{l}")
        dh, dg1 = _in_bwd(dh1, h0, g1, dq, dk, dv, drest, wqkv[l], wrest[l], name=f"in_bwd_{l}")
        d_wq = _mm_tn(hn1, dq, name=f"dw_q_{l}")
        d_wk = _mm_tn(hn1, dk, name=f"dw_k_{l}")
        d_wv = _mm_tn(hn1, dv, name=f"dw_v_{l}")
        d_wr = _mm_tn(hn1, drest, name=f"dw_rest_{l}")
        d_win = jnp.concatenate([d_wq, d_wk, d_wv, d_wr[:, 5 * D_CONF:5 * D_CONF + N_HEADS], d_wr[:, :5 * D_CONF]], axis=1)
        grads[l] = dict(w_in=d_win, w_out=d_wout, w_mlp1=d_w1, w_mlp2=d_w2, mix_norm_g=dg1[0], mlp_norm_g=dg2[0],
                        w_conf_dw=d_wdw[:CONF_K], w_sc_conv=d_wsc[:SC_K], w_conf_pw=d_wpw, b_conf_dw=d_vec[0],
                        conf_ln_g=d_vec[1], conf_ln_b=d_vec[2], b_conf_pw=d_vec[3], b_forget=d_vec[4, :N_HEADS])
    grad_x = dh[N_META:N_META + seq][None]
    d_meta = dh[:N_META]

    def stack(k):
        return jnp.stack([grads[l][k] for l in range(depth)])

    def col_shards(a):
        d, rr, cc = a.shape
        return a.reshape(d, rr, 4, cc // 4).transpose(2, 0, 1, 3).reshape(4, d * rr, cc // 4)

    def row_shards(a):
        d, rr, cc = a.shape
        return a.reshape(d, 4, rr // 4, cc).transpose(1, 0, 2, 3).reshape(4, d * rr // 4, cc)

    big = [col_shards(stack("w_in")), row_shards(stack("w_out")), col_shards(stack("w_mlp1")), row_shards(stack("w_mlp2"))]
    rep = [stack("mix_norm_g"), stack("b_forget"), stack("b_conf_dw"), stack("conf_ln_g"), stack("conf_ln_b"),
           stack("b_conf_pw"), stack("mlp_norm_g"), d_final_g[0]]
    rep_shapes = [a.shape for a in rep]
    sh_dw, sh_pw, sh_sc = col_shards(stack("w_conf_dw")), row_shards(stack("w_conf_pw")), col_shards(stack("w_sc_conv"))
    sh_meta = d_meta.reshape(N_META, 4, D_MODEL // 4).transpose(1, 0, 2)
    n_rep = sum(a.size for a in rep)
    gs_rows = -(-(n_small + n_rep) // 1024 // 8) * 8
    gsmall = jnp.stack([_flat_rows([sh_dw[j], sh_pw[j], sh_sc[j], sh_meta[j]] + rep, 1024, gs_rows) for j in range(4)])
    recv = _chip_exchange([b.astype(BF16) for b in big] + [gsmall], name="grads_exchange")
    own = [lax.dynamic_index_in_dim(b, me, 0, keepdims=False) for b in big + [gsmall]]
    sums = [_shard_sum(own[i], recv[i], name=f"shard_sum_{i}", slot_order=(i == 4)) for i in range(5)]
    sibs = _sibling_exchange(sums, name="sibling_exchange")

    w_small = _flat_rows([w_conf_dw, w_conf_pw, w_sc_conv, meta_tokens, mix_norm_g, b_forget, b_conf_dw, conf_ln_g,
                          conf_ln_b, b_conf_pw, mlp_norm_g, final_norm_g], 1024, gs_rows)
    m_small = _flat_rows([m_w_conf_dw, m_w_conf_pw, m_w_sc_conv, m_meta_tokens, m_mix_norm_g, m_b_forget, m_b_conf_dw,
                          m_conf_ln_g, m_conf_ln_b, m_b_conf_pw, m_mlp_norm_g, m_final_norm_g], 1024, gs_rows)
    v_small = _flat_rows([v_w_conf_dw, v_w_conf_pw, v_w_sc_conv, v_meta_tokens, v_mix_norm_g, v_b_forget, v_b_conf_dw,
                          v_conf_ln_g, v_conf_ln_b, v_b_conf_pw, v_mlp_norm_g, v_final_norm_g], 1024, gs_rows)
    triples = [(w_in, m_w_in, v_w_in), (w_out, m_w_out, v_w_out), (w_mlp1, m_w_mlp1, v_w_mlp1),
               (w_mlp2, m_w_mlp2, v_w_mlp2), (w_small, m_small, v_small)]
    res = []
    for i, (w, m, v) in enumerate(triples):
        shp = w.shape
        two_d = sums[i].shape
        out4 = _adamw(w.reshape(two_d), m.reshape(two_d), v.reshape(two_d), sums[i], sibs[i], name=f"adamw_{i}")
        res.append([o.reshape(shp) for o in out4])
    small_out = [_unflat(res[4][k], small_shapes + rep_shapes) for k in range(4)]

    names = ["meta_tokens", "mix_norm_g", "w_in", "b_forget", "w_conf_dw", "b_conf_dw", "conf_ln_g", "conf_ln_b",
             "w_conf_pw", "b_conf_pw", "w_sc_conv", "w_out", "mlp_norm_g", "w_mlp1", "w_mlp2", "final_norm_g"]
    small_order = ["w_conf_dw", "w_conf_pw", "w_sc_conv", "meta_tokens", "mix_norm_g", "b_forget", "b_conf_dw",
                   "conf_ln_g", "conf_ln_b", "b_conf_pw", "mlp_norm_g", "final_norm_g"]
    big_order = {"w_in": 0, "w_out": 1, "w_mlp1": 2, "w_mlp2": 3}
    outs = []
    for k in range(4):
        for nme in names:
            outs.append(res[big_order[nme]][k] if nme in big_order else small_out[k][small_order.index(nme)])
    return (loss, grad_x, *outs)
```

```python
import jax
import jax.numpy as jnp
from jax import lax
from jax.experimental import pallas as pl
from jax.experimental.pallas import tpu as pltpu

F32, BF16 = jnp.float32, jnp.bfloat16
D_MODEL = 1024
D_ATTN = 512
D_CONF = 256
N_HEADS = 8
HEAD_DIM = 64
N_PAIRS = N_HEADS // 2
PAIR_W = 2 * HEAD_DIM
N_META = 16
D_FF = 4096
CONF_K = 31
SC_K = 3
HALO = 32
N_F_PAD = 128
N_REST = 5 * D_CONF + N_F_PAD
F_BLK = 5 * D_CONF // N_F_PAD
EPS = 1e-6
NEG = -1e30
BIG_TILE = 640
SMALL_TILE = 128
MLP_TILE_DIV = 2
V7X_VMEM_BYTES = 64 * 1024 * 1024
ADAM_LR, ADAM_B1, ADAM_B2, ADAM_EPS, ADAM_WD, ADAM_STEP = 0.001, 0.9, 0.999, 1e-08, 0.01, 10
MESH = pl.DeviceIdType.MESH
HIGHEST = lax.Precision.HIGHEST
SDS = jax.ShapeDtypeStruct


def _cparams(sem=None, vmem_mb=None):
    kw = {}
    if sem is not None:
        kw["dimension_semantics"] = sem
    if vmem_mb is not None:
        assert vmem_mb * 1024 * 1024 <= V7X_VMEM_BYTES
        kw["vmem_limit_bytes"] = vmem_mb * 1024 * 1024
    return pltpu.CompilerParams(**kw)


def _dot(a, b):
    return jnp.dot(a, b, preferred_element_type=F32)


def _dot_nt(a, b):
    return lax.dot_general(a, b, (((1,), (1,)), ((), ())), preferred_element_type=F32)


def _dot_tn(a, b):
    return lax.dot_general(a, b, (((0,), (0,)), ((), ())), preferred_element_type=F32)


def _sigmoid(x):
    return 1.0 / (1.0 + jnp.exp(-x))


def _rms_fwd(x, g):
    r = lax.rsqrt(jnp.mean(x * x, axis=-1, keepdims=True) + EPS)
    xn = x * r
    return r, xn, xn * g


def _rms_bwd(r, xn, g, dy):
    u = dy * g
    return r * (u - xn * jnp.mean(u * xn, axis=-1, keepdims=True))


def _colsum(x):
    return jnp.sum(x, axis=0, keepdims=True)


def _row_tile(lp):
    return BIG_TILE if lp % BIG_TILE == 0 else SMALL_TILE


def _padded_len(l):
    t = BIG_TILE if l >= BIG_TILE else SMALL_TILE
    return -(-l // t) * t


def _weights_all_gather(arrs, name):
    n = len(arrs)
    halves = [a.shape[0] // 2 for a in arrs]
    assert all(a.shape[0] == 2 * h and h % 8 == 0 for a, h in zip(arrs, halves))

    def body(*refs):
        ins, outs = refs[:n], refs[n:2 * n]
        ici_send, ici_recv, d2d_send, d2d_recv, local_sems = refs[2 * n:]
        x, y, c = lax.axis_index("x"), lax.axis_index("y"), lax.axis_index("c")
        me = 2 * x + y
        sibling = (x, y, 1 - c)
        chips = [(1 - x, y), (x, 1 - y), (1 - x, 1 - y)]

        def half(ref, i, which):
            return ref.at[pl.ds(pl.multiple_of(which * halves[i], 8), halves[i])]

        def copy(src, dst, send_sem, recv_sem, to):
            return pltpu.make_async_remote_copy(src_ref=src, dst_ref=dst, send_sem=send_sem, recv_sem=recv_sem,
                                                device_id=to, device_id_type=MESH)

        local, over_ici, landed, passed_on, from_sibling = [], [], [], [], []
        for i in range(n):
            local.append(pltpu.make_async_copy(ins[i], outs[i].at[me], local_sems.at[i]))
            for j, (px, py) in enumerate(chips):
                k = 3 * i + j
                peer = 2 * px + py
                mine, theirs = half(outs[i].at[me], i, c), half(outs[i].at[peer], i, c)
                over_ici.append(copy(half(ins[i], i, c), mine, ici_send.at[k], ici_recv.at[k], (px, py, c)))
                landed.append(copy(half(ins[i], i, c), theirs, ici_send.at[k], ici_recv.at[k], (px, py, c)))
                passed_on.append(copy(theirs, theirs, d2d_send.at[k], d2d_recv.at[k], sibling))
                other = half(outs[i].at[peer], i, 1 - c)
                from_sibling.append(copy(other, other, d2d_send.at[k], d2d_recv.at[k], sibling))
        for cp in local + over_ici:
            cp.start()
        for arrived, onward in zip(landed, passed_on):
            arrived.wait_recv()
            onward.start()
        for cp in from_sibling:
            cp.wait_recv()
        for cp in over_ici + passed_on:
            cp.wait_send()
        for cp in local:
            cp.wait()

    any_spec = pl.BlockSpec(memory_space=pl.ANY)
    return pl.pallas_call(
        body, name=name, out_shape=[SDS((4,) + a.shape, a.dtype) for a in arrs],
        in_specs=[any_spec] * n, out_specs=[any_spec] * n,
        scratch_shapes=[pltpu.SemaphoreType.DMA((3 * n,)) for _ in range(4)] + [pltpu.SemaphoreType.DMA((n,))],
    )(*arrs)


def _chip_exchange(arrs, name):
    n = len(arrs)

    def body(*refs):
        ins, outs = refs[:n], refs[n:2 * n]
        send_sems, recv_sems, local_sems = refs[2 * n:]
        x, y, c = lax.axis_index("x"), lax.axis_index("y"), lax.axis_index("c")
        me = 2 * x + y
        chips = [(1 - x, y), (x, 1 - y), (1 - x, 1 - y)]
        local, sends, recvs = [], [], []
        for i in range(n):
            local.append(pltpu.make_async_copy(ins[i].at[me], outs[i].at[me], local_sems.at[i]))
            for j, (px, py) in enumerate(chips):
                k = 3 * i + j
                peer = 2 * px + py
                src = ins[i].at[peer]
                sends.append(pltpu.make_async_remote_copy(
                    src_ref=src, dst_ref=outs[i].at[me], send_sem=send_sems.at[k], recv_sem=recv_sems.at[k],
                    device_id=(px, py, c), device_id_type=MESH))
                recvs.append(pltpu.make_async_remote_copy(
                    src_ref=src, dst_ref=outs[i].at[peer], send_sem=send_sems.at[k], recv_sem=recv_sems.at[k],
                    device_id=(px, py, c), device_id_type=MESH))
        for cp in local + sends:
            cp.start()
        for cp in recvs:
            cp.wait_recv()
        for cp in sends:
            cp.wait_send()
        for cp in local:
            cp.wait()

    any_spec = pl.BlockSpec(memory_space=pl.ANY)
    return pl.pallas_call(
        body, name=name, out_shape=[SDS(a.shape, a.dtype) for a in arrs],
        in_specs=[any_spec] * n, out_specs=[any_spec] * n,
        scratch_shapes=[pltpu.SemaphoreType.DMA((3 * n,)), pltpu.SemaphoreType.DMA((3 * n,)),
                        pltpu.SemaphoreType.DMA((n,))],
    )(*arrs)


def _sibling_exchange(arrs, name):
    n = len(arrs)

    def body(*refs):
        ins, outs = refs[:n], refs[n:2 * n]
        send_sems, recv_sems = refs[2 * n:]
        sib = (lax.axis_index("x"), lax.axis_index("y"), 1 - lax.axis_index("c"))
        cps = [pltpu.make_async_remote_copy(src_ref=ins[i], dst_ref=outs[i], send_sem=send_sems.at[i],
                                            recv_sem=recv_sems.at[i], device_id=sib, device_id_type=MESH)
               for i in range(n)]
        for cp in cps:
            cp.start()
        for cp in cps:
            cp.wait_recv()
        for cp in cps:
            cp.wait_send()

    any_spec = pl.BlockSpec(memory_space=pl.ANY)
    return pl.pallas_call(
        body, name=name, out_shape=[SDS(a.shape, a.dtype) for a in arrs],
        in_specs=[any_spec] * n, out_specs=[any_spec] * n,
        scratch_shapes=[pltpu.SemaphoreType.DMA((n,)), pltpu.SemaphoreType.DMA((n,))],
    )(*arrs)


def _proj_fwd(h, g, wqkv, wrest, name):
    lp = h.shape[0]
    tm = _row_tile(lp)

    def body(h_ref, g_ref, wq_ref, wr_ref, hn_ref, qkv_ref, rest_ref):
        _, _, y = _rms_fwd(h_ref[...], g_ref[...])
        hn = y.astype(BF16)
        hn_ref[...] = hn
        qkv = _dot(hn, wq_ref[...])
        qkv_ref[:, :D_ATTN] = (qkv[:, :D_ATTN] * (HEAD_DIM ** -0.5)).astype(BF16)
        qkv_ref[:, D_ATTN:] = qkv[:, D_ATTN:].astype(BF16)
        rest_ref[...] = _dot(hn, wr_ref[...])

    row = lambda w: pl.BlockSpec((tm, w), lambda i: (i, 0))
    full = lambda a: pl.BlockSpec(a.shape, lambda i: (0, 0))
    return pl.pallas_call(
        body, name=name, grid=(lp // tm,),
        in_specs=[row(D_MODEL), full(g), full(wqkv), full(wrest)],
        out_specs=[row(D_MODEL), row(3 * D_ATTN), row(N_REST)],
        out_shape=[SDS((lp, D_MODEL), BF16), SDS((lp, 3 * D_ATTN), BF16), SDS((lp, N_REST), F32)],
        compiler_params=_cparams(("parallel",), 48),
    )(h, g, wqkv, wrest)


def _mm_tn(a, b, name):
    kk, m = a.shape
    n = b.shape[1]
    tk = _row_tile(kk)
    tmo = min(m, 1024)
    tno = n if n <= 1536 else 1024

    def body(a_ref, b_ref, o_ref):
        @pl.when(pl.program_id(2) == 0)
        def _():
            o_ref[...] = jnp.zeros_like(o_ref)
        o_ref[...] += _dot_tn(a_ref[...], b_ref[...].astype(BF16))

    return pl.pallas_call(
        body, name=name, grid=(m // tmo, n // tno, kk // tk),
        in_specs=[pl.BlockSpec((tk, tmo), lambda i, j, k: (k, i)), pl.BlockSpec((tk, tno), lambda i, j, k: (k, j))],
        out_specs=pl.BlockSpec((tmo, tno), lambda i, j, k: (i, j)),
        out_shape=SDS((m, n), F32),
        compiler_params=_cparams(("parallel", "parallel", "arbitrary"), 48),
    )(a, b)


def _mix_out(h, attn, cs, wa, wc, name):
    lp = h.shape[0]
    tm = _row_tile(lp)

    def body(h_ref, at_ref, cs_ref, wa_ref, wc_ref, o_ref):
        o_ref[...] = h_ref[...] + _dot(at_ref[...], wa_ref[...]) + _dot(cs_ref[...], wc_ref[...])

    row = lambda w: pl.BlockSpec((tm, w), lambda i: (i, 0))
    full = lambda a: pl.BlockSpec(a.shape, lambda i: (0, 0))
    return pl.pallas_call(
        body, name=name, grid=(lp // tm,),
        in_specs=[row(D_MODEL), row(D_ATTN), row(D_ATTN), full(wa), full(wc)],
        out_specs=row(D_MODEL), out_shape=SDS((lp, D_MODEL), F32),
        compiler_params=_cparams(("parallel",), 48),
    )(h, attn, cs, wa, wc)


def _mlp_fwd(h, g, w1, w2, name):
    lp = h.shape[0]
    tm = _row_tile(lp) // MLP_TILE_DIV

    def body(h_ref, g_ref, w1_ref, w2_ref, hn_ref, a_ref, r_ref, o_ref):
        x = h_ref[...]
        _, _, y = _rms_fwd(x, g_ref[...])
        hn = y.astype(BF16)
        hn_ref[...] = hn
        a = _dot(hn, w1_ref[...])
        a_ref[...] = a
        r = jnp.square(jnp.maximum(a, 0.0)).astype(BF16)
        r_ref[...] = r
        o_ref[...] = x + _dot(r, w2_ref[...])

    row = lambda w: pl.BlockSpec((tm, w), lambda i: (i, 0))
    vmem = pl.BlockSpec(memory_space=pltpu.VMEM)
    return pl.pallas_call(
        body, name=name, grid=(lp // tm,),
        in_specs=[row(D_MODEL), pl.BlockSpec(g.shape, lambda i: (0, 0)), vmem, vmem],
        out_specs=[row(D_MODEL), row(D_FF), row(D_FF), row(D_MODEL)],
        out_shape=[SDS((lp, D_MODEL), BF16), SDS((lp, D_FF), F32), SDS((lp, D_FF), BF16), SDS((lp, D_MODEL), F32)],
        compiler_params=_cparams(("parallel",), 60),
    )(h, g, w1, w2)


def _mlp_bwd(dh2, h1, g, a, w1, w2, name):
    lp = h1.shape[0]
    tm = _row_tile(lp) // MLP_TILE_DIV

    def body(dy_ref, h_ref, g_ref, a_ref, w1_ref, w2_ref, da_ref, dh_ref, dg_ref):
        @pl.when(pl.program_id(0) == 0)
        def _():
            dg_ref[...] = jnp.zeros_like(dg_ref)
        dy = dy_ref[...]
        dr = _dot_nt(dy.astype(BF16), w2_ref[...])
        da = (dr * (2.0 * jnp.maximum(a_ref[...], 0.0))).astype(BF16)
        da_ref[...] = da
        dhn = _dot_nt(da, w1_ref[...])
        r, xn, _ = _rms_fwd(h_ref[...], g_ref[...])
        dg_ref[...] += _colsum(dhn * xn)
        dh_ref[...] = dy + _rms_bwd(r, xn, g_ref[...], dhn)

    row = lambda w: pl.BlockSpec((tm, w), lambda i: (i, 0))
    vmem = pl.BlockSpec(memory_space=pltpu.VMEM)
    vec = pl.BlockSpec((1, D_MODEL), lambda i: (0, 0))
    return pl.pallas_call(
        body, name=name, grid=(lp // tm,),
        in_specs=[row(D_MODEL), row(D_MODEL), vec, row(D_FF), vmem, vmem],
        out_specs=[row(D_FF), row(D_MODEL), vec],
        out_shape=[SDS((lp, D_FF), BF16), SDS((lp, D_MODEL), F32), SDS((1, D_MODEL), F32)],
        compiler_params=_cparams(("arbitrary",), 60),
    )(dh2, h1, g, a, w1, w2)


def _mix_bwd(dh1, wa, wc, attn, seg, name):
    lp = dh1.shape[0]
    tm = _row_tile(lp)

    def body(dy_ref, wa_ref, wc_ref, o_ref, seg_ref, do_ref, dcs_ref, dl_ref):
        dy = dy_ref[...].astype(BF16)
        do = _dot_nt(dy, wa_ref[...])
        do_ref[...] = do.astype(BF16)
        dcs_ref[...] = _dot_nt(dy, wc_ref[...])
        dl_ref[...] = jnp.dot(do * o_ref[...].astype(F32), seg_ref[...], precision=HIGHEST,
                              preferred_element_type=F32)

    row = lambda w: pl.BlockSpec((tm, w), lambda i: (i, 0))
    full = lambda x: pl.BlockSpec(x.shape, lambda i: (0, 0))
    return pl.pallas_call(
        body, name=name, grid=(lp // tm,),
        in_specs=[row(D_MODEL), full(wa), full(wc), row(D_ATTN), full(seg)],
        out_specs=[row(D_ATTN), row(D_ATTN), row(D_ATTN)],
        out_shape=[SDS((lp, D_ATTN), BF16), SDS((lp, D_ATTN), F32), SDS((lp, D_ATTN), F32)],
        compiler_params=_cparams(("parallel",), 48),
    )(dh1, wa, wc, attn, seg)


def _in_bwd(dh1, h, g, dq, dk, dv, drest, wqkv, wrest, name):
    lp = h.shape[0]
    tm = _row_tile(lp)

    def body(dy_ref, h_ref, g_ref, dq_ref, dk_ref, dv_ref, dr_ref, wq_ref, wr_ref, dh_ref, dg_ref):
        @pl.when(pl.program_id(0) == 0)
        def _():
            dg_ref[...] = jnp.zeros_like(dg_ref)
        dhn = (_dot_nt(dq_ref[...], wq_ref[:, 0:D_ATTN]) + _dot_nt(dk_ref[...], wq_ref[:, D_ATTN:2 * D_ATTN])
               + _dot_nt(dv_ref[...], wq_ref[:, 2 * D_ATTN:]) + _dot_nt(dr_ref[...], wr_ref[...]))
        r, xn, _ = _rms_fwd(h_ref[...], g_ref[...])
        dg_ref[...] += _colsum(dhn * xn)
        dh_ref[...] = dy_ref[...] + _rms_bwd(r, xn, g_ref[...], dhn)

    row = lambda w: pl.BlockSpec((tm, w), lambda i: (i, 0))
    full = lambda x: pl.BlockSpec(x.shape, lambda i: (0, 0))
    vec = pl.BlockSpec((1, D_MODEL), lambda i: (0, 0))
    return pl.pallas_call(
        body, name=name, grid=(lp // tm,),
        in_specs=[row(D_MODEL), row(D_MODEL), vec, row(D_ATTN), row(D_ATTN), row(D_ATTN), row(N_REST),
                  full(wqkv), full(wrest)],
        out_specs=[row(D_MODEL), vec],
        out_shape=[SDS((lp, D_MODEL), F32), SDS((1, D_MODEL), F32)],
        compiler_params=_cparams(("arbitrary",), 48),
    )(dh1, h, g, dq, dk, dv, drest, wqkv, wrest)


def _loss_head(h, target, g, seq, name):
    lp = h.shape[0]
    tm = _row_tile(lp)

    def body(h_ref, t_ref, g_ref, dh_ref, loss_ref, dg_ref):
        i = pl.program_id(0)

        @pl.when(i == 0)
        def _():
            dg_ref[...] = jnp.zeros_like(dg_ref)
            loss_ref[...] = jnp.zeros_like(loss_ref)
        r, xn, y = _rms_fwd(h_ref[...], g_ref[...])
        rows = i * tm + lax.broadcasted_iota(jnp.int32, (tm, 1), 0)
        real = (rows >= N_META) & (rows < N_META + seq)
        e = jnp.where(real, y - t_ref[...], 0.0)
        loss_ref[...] += 0.5 * jnp.sum(jnp.mean(e * e, axis=-1, keepdims=True), axis=0, keepdims=True)
        dy = e * (1.0 / D_MODEL)
        dg_ref[...] += _colsum(dy * xn)
        dh_ref[...] = _rms_bwd(r, xn, g_ref[...], dy)

    row = pl.BlockSpec((tm, D_MODEL), lambda i: (i, 0))
    vec = pl.BlockSpec((1, D_MODEL), lambda i: (0, 0))
    return pl.pallas_call(
        body, name=name, grid=(lp // tm,),
        in_specs=[row, row, vec],
        out_specs=[row, pl.BlockSpec((1, 1), lambda i: (0, 0)), vec],
        out_shape=[SDS((lp, D_MODEL), F32), SDS((1, 1), F32), SDS((1, D_MODEL), F32)],
        compiler_params=_cparams(("arbitrary",), 48),
    )(h, target, g)


def _shift_down(xcat, j):
    return (xcat if j == 0 else pltpu.roll(xcat, j, 0))[HALO:, :]


def _shift_up(xcat, j, tm):
    n = xcat.shape[0]
    return (xcat if j == 0 else pltpu.roll(xcat, n - j, 0))[:tm, :]


def _layernorm_parts(x):
    mu = jnp.mean(x, axis=-1, keepdims=True)
    xc = x - mu
    rstd = lax.rsqrt(jnp.mean(xc * xc, axis=-1, keepdims=True) + EPS)
    return rstd, xc * rstd


def _log_sigmoid(z):
    return jnp.minimum(z, 0.0) - jnp.log(1.0 + jnp.exp(-jnp.abs(z)))


def _conv_fwd(rest, wdw, vec, wpw, wsc, name):
    lp = rest.shape[0]
    tm = _row_tile(lp)
    per = tm // HALO

    def body(a_ref, g_ref, ah_ref, gh_ref, b_ref, c_ref, u_ref, ch_ref, uh_ref, f_ref,
             wdw_ref, vec_ref, wpw_ref, wsc_ref, cs_ref, dwc_ref, cv_ref, cc_ref, carry):
        i = pl.program_id(0)
        first = i == 0

        @pl.when(first)
        def _():
            carry[...] = jnp.zeros_like(carry)
        glu = a_ref[...] * _sigmoid(g_ref[...])
        glu_h = jnp.where(first, 0.0, ah_ref[...] * _sigmoid(gh_ref[...]))
        xcat = jnp.concatenate([glu_h, glu], axis=0)
        acc = jnp.zeros((tm, D_CONF), F32) + vec_ref[0:1, :]
        for k in range(CONF_K):
            acc = acc + wdw_ref[k:k + 1, :] * _shift_down(xcat, CONF_K - 1 - k)
        dwc_ref[...] = acc
        _, y = _layernorm_parts(acc)
        ln = y * vec_ref[1:2, :] + vec_ref[2:3, :]
        sw = ln * _sigmoid(ln)
        conf = _dot(sw.astype(BF16), wpw_ref[...]) + vec_ref[3:4, :]
        p = c_ref[...] * u_ref[...]
        p_h = jnp.where(first, 0.0, ch_ref[...] * uh_ref[...])
        pcat = jnp.concatenate([p_h, p], axis=0)
        cv = jnp.zeros((tm, D_CONF), F32)
        for k in range(SC_K):
            cv = cv + wsc_ref[k:k + 1, :] * _shift_down(pcat, SC_K - 1 - k)
        cv_ref[...] = cv
        cs_ref[:, :D_CONF] = conf.astype(BF16)
        cs_ref[:, D_CONF:] = (b_ref[...] * cv).astype(BF16)
        logf = _log_sigmoid(f_ref[...] + vec_ref[4:5, :N_F_PAD])
        tri = (lax.broadcasted_iota(jnp.int32, (tm, tm), 0) >= lax.broadcasted_iota(jnp.int32, (tm, tm), 1))
        c = jnp.dot(tri.astype(F32), logf, precision=HIGHEST, preferred_element_type=F32) + carry[0:1, :]
        cc_ref[...] = c
        carry[0:1, :] = c[tm - 1:tm, :]

    cur = lambda j: pl.BlockSpec((tm, D_CONF), lambda i: (i, j))
    prev = lambda j: pl.BlockSpec((HALO, D_CONF), lambda i: (jnp.maximum(i * per - 1, 0), j))
    full = lambda x: pl.BlockSpec(x.shape, lambda i: (0, 0))
    return pl.pallas_call(
        body, name=name, grid=(lp // tm,),
        in_specs=[cur(0), cur(1), prev(0), prev(1), cur(2), cur(3), cur(4), prev(3), prev(4),
                  pl.BlockSpec((tm, N_F_PAD), lambda i: (i, F_BLK)),
                  full(wdw), full(vec), full(wpw), full(wsc)],
        out_specs=[pl.BlockSpec((tm, 2 * D_CONF), lambda i: (i, 0)), cur(0), cur(0),
                   pl.BlockSpec((tm, N_F_PAD), lambda i: (i, 0))],
        out_shape=[SDS((lp, 2 * D_CONF), BF16), SDS((lp, D_CONF), F32), SDS((lp, D_CONF), F32),
                   SDS((lp, N_F_PAD), F32)],
        scratch_shapes=[pltpu.VMEM((8, N_F_PAD), F32)],
        compiler_params=_cparams(("arbitrary",), 48),
    )(rest, rest, rest, rest, rest, rest, rest, rest, rest, rest, wdw, vec, wpw, wsc)


def _conv_bwd(dcs, dwc, cv, rest, dccol, wdw, vec, wpw, wsc, name):
    lp = rest.shape[0]
    tm = _row_tile(lp)
    per = tm // HALO
    nt = lp // tm
    n_halo_blocks = lp // HALO

    def body(dcf_ref, dsc_ref, dcfn_ref, dscn_ref, dwc_ref, dwcn_ref, cv_ref,
             a_ref, g_ref, ah_ref, gh_ref, b_ref, bn_ref, c_ref, u_ref, ch_ref, uh_ref, f_ref, dcc_ref,
             wdw_ref, vec_ref, wpw_ref, wsc_ref,
             dr_ref, dwdw_ref, dwsc_ref, dwpw_ref, dvec_ref, carry):
        i = pl.program_id(0)
        first = i == nt - 1
        last = i == 0

        @pl.when(i == 0)
        def _():
            carry[...] = jnp.zeros_like(carry)
            dwdw_ref[...] = jnp.zeros_like(dwdw_ref)
            dwsc_ref[...] = jnp.zeros_like(dwsc_ref)
            dwpw_ref[...] = jnp.zeros_like(dwpw_ref)
            dvec_ref[...] = jnp.zeros_like(dvec_ref)

        ln_g, ln_b = vec_ref[1:2, :], vec_ref[2:3, :]

        def ln_bwd(dconf, x):
            rstd, y = _layernorm_parts(x)
            ln = y * ln_g + ln_b
            sg = _sigmoid(ln)
            dsw = _dot_nt(dconf.astype(BF16), wpw_ref[...])
            dln = dsw * (sg * (1.0 + ln * (1.0 - sg)))
            dy = dln * ln_g
            ddw = rstd * (dy - jnp.mean(dy, axis=-1, keepdims=True) - y * jnp.mean(dy * y, axis=-1, keepdims=True))
            return ln * sg, y, dln, ddw

        dconf = dcf_ref[...]
        sw, y, dln, ddw = ln_bwd(dconf, dwc_ref[...])
        _, _, _, ddw_n = ln_bwd(dcfn_ref[...], dwcn_ref[...])
        ddw_n = jnp.where(last, 0.0, ddw_n)
        dvec_ref[0:1, :] += _colsum(ddw)
        dvec_ref[1:2, :] += _colsum(dln * y)
        dvec_ref[2:3, :] += _colsum(dln)
        dvec_ref[3:4, :] += _colsum(dconf)
        dwpw_ref[...] += _dot_tn(sw.astype(BF16), dconf.astype(BF16))

        a, sg = a_ref[...], _sigmoid(g_ref[...])
        glu = a * sg
        glu_h = jnp.where(first, 0.0, ah_ref[...] * _sigmoid(gh_ref[...]))
        xcat = jnp.concatenate([glu_h, glu], axis=0)
        dcat = jnp.concatenate([ddw, ddw_n], axis=0)
        dglu = jnp.zeros((tm, D_CONF), F32)
        for k in range(CONF_K):
            j = CONF_K - 1 - k
            dwdw_ref[k:k + 1, :] += _colsum(ddw * _shift_down(xcat, j))
            dglu = dglu + wdw_ref[k:k + 1, :] * _shift_up(dcat, j, tm)
        dr_ref[:, 0:D_CONF] = (dglu * sg).astype(BF16)
        dr_ref[:, D_CONF:2 * D_CONF] = (dglu * a * sg * (1.0 - sg)).astype(BF16)

        dsc = dsc_ref[...]
        b, c, u = b_ref[...], c_ref[...], u_ref[...]
        dcv = dsc * b
        dcv_n = jnp.where(last, 0.0, dscn_ref[...] * bn_ref[...])
        p_h = jnp.where(first, 0.0, ch_ref[...] * uh_ref[...])
        pcat = jnp.concatenate([p_h, c * u], axis=0)
        dccat = jnp.concatenate([dcv, dcv_n], axis=0)
        dp = jnp.zeros((tm, D_CONF), F32)
        for k in range(SC_K):
            j = SC_K - 1 - k
            dwsc_ref[k:k + 1, :] += _colsum(dcv * _shift_down(pcat, j))
            dp = dp + wsc_ref[k:k + 1, :] * _shift_up(dccat, j, tm)
        dr_ref[:, 2 * D_CONF:3 * D_CONF] = (dsc * cv_ref[...]).astype(BF16)
        dr_ref[:, 3 * D_CONF:4 * D_CONF] = (dp * u).astype(BF16)
        dr_ref[:, 4 * D_CONF:5 * D_CONF] = (dp * c).astype(BF16)

        tri = (lax.broadcasted_iota(jnp.int32, (tm, tm), 0) <= lax.broadcasted_iota(jnp.int32, (tm, tm), 1))
        dlogf = jnp.dot(tri.astype(F32), dcc_ref[...], precision=HIGHEST, preferred_element_type=F32) + carry[0:1, :]
        carry[0:1, :] = dlogf[0:1, :]
        dz = dlogf * _sigmoid(-(f_ref[...] + vec_ref[4:5, :N_F_PAD]))
        dvec_ref[4:5, 0:N_F_PAD] += _colsum(dz)
        dr_ref[:, 5 * D_CONF:] = dz.astype(BF16)

    rt = lambda i: nt - 1 - i
    cur = lambda j: pl.BlockSpec((tm, D_CONF), lambda i: (rt(i), j))
    prev = lambda j: pl.BlockSpec((HALO, D_CONF), lambda i: (jnp.maximum(rt(i) * per - 1, 0), j))
    nxt = lambda j: pl.BlockSpec((HALO, D_CONF), lambda i: (jnp.minimum((rt(i) + 1) * per, n_halo_blocks - 1), j))
    full = lambda x: pl.BlockSpec(x.shape, lambda i: (0, 0))
    acc = lambda r, w: pl.BlockSpec((r, w), lambda i: (0, 0))
    return pl.pallas_call(
        body, name=name, grid=(nt,),
        in_specs=[cur(0), cur(1), nxt(0), nxt(1), cur(0), nxt(0), cur(0),
                  cur(0), cur(1), prev(0), prev(1), cur(2), nxt(2), cur(3), cur(4), prev(3), prev(4),
                  pl.BlockSpec((tm, N_F_PAD), lambda i: (rt(i), F_BLK)),
                  pl.BlockSpec((tm, N_F_PAD), lambda i: (rt(i), 0)),
                  full(wdw), full(vec), full(wpw), full(wsc)],
        out_specs=[pl.BlockSpec((tm, N_REST), lambda i: (rt(i), 0)), acc(32, D_CONF), acc(8, D_CONF),
                   acc(D_CONF, D_CONF), acc(8, D_CONF)],
        out_shape=[SDS((lp, N_REST), BF16), SDS((32, D_CONF), F32), SDS((8, D_CONF), F32),
                   SDS((D_CONF, D_CONF), F32), SDS((8, D_CONF), F32)],
        scratch_shapes=[pltpu.VMEM((8, N_F_PAD), F32)],
        compiler_params=_cparams(("arbitrary",), 48),
    )(dcs, dcs, dcs, dcs, dwc, dwc, cv, rest, rest, rest, rest, rest, rest, rest, rest, rest, rest, rest, dccol,
      wdw, vec, wpw, wsc)


def _causal_tiles(n, key_major):
    if key_major:
        pairs = [(q, k) for k in range(n) for q in range(k, n)]
    else:
        pairs = [(q, k) for q in range(n) for k in range(q + 1)]
    return (jnp.asarray([p[0] for p in pairs], jnp.int32), jnp.asarray([p[1] for p in pairs], jnp.int32))


def _attn_fwd(qkv, crow, cpair, name):
    lp = qkv.shape[0]
    t = _row_tile(lp)
    nq = lp // t

    qtab, ktab = _causal_tiles(nq, key_major=False)

    def body(qt_ref, kt_ref, q_ref, k_ref, v_ref, c_ref, cq_ref, o_ref, lse_ref, m_scr, l_scr, acc_scr):
        qi, ki = qt_ref[pl.program_id(1)], kt_ref[pl.program_id(1)]

        @pl.when(ki == 0)
        def _():
            lane = lax.broadcasted_iota(jnp.int32, (t, PAIR_W), 1)
            m_scr[...] = jnp.full_like(m_scr, NEG)
            l_scr[...] = jnp.where(lane < 2, 0.0, 1.0)
            acc_scr[...] = jnp.zeros_like(acc_scr)

        def step(diagonal):
            q, k, v = q_ref[...], k_ref[...], v_ref[...]
            lane = lax.broadcasted_iota(jnp.int32, (t, PAIR_W), 1)
            lo = lane < HEAD_DIM
            m_all, l_all = m_scr[...], l_scr[...]
            pv, alpha, m_out, l_out = [], [], [], []
            for j in range(2):
                sel = lo if j == 0 else jnp.logical_not(lo)
                qj = jnp.where(sel, q, jnp.zeros_like(q))
                cq = cq_ref[:, j:j + 1]
                m_run, l_run = m_all[:, j:j + 1], l_all[:, j:j + 1]
                al_run = jnp.ones((t, 1), F32)
                pv_run = jnp.zeros((t, PAIR_W), F32)
                for kc in range(t // PAIR_W):
                    cols = slice(kc * PAIR_W, (kc + 1) * PAIR_W)
                    s = _dot_nt(qj, k[cols, :]) + (cq - c_ref[j:j + 1, cols])
                    if diagonal:
                        causal = (lax.broadcasted_iota(jnp.int32, (t, PAIR_W), 0)
                                  >= kc * PAIR_W + lax.broadcasted_iota(jnp.int32, (t, PAIR_W), 1))
                        s = jnp.where(causal, s, -jnp.inf)
                    m_new = jnp.maximum(m_run, jnp.max(s, axis=1, keepdims=True))
                    al = jnp.exp(m_run - m_new)
                    p = jnp.exp(s - m_new)
                    l_run = al * l_run + jnp.sum(p, axis=1, keepdims=True)
                    pv_run = al * pv_run + _dot(p.astype(BF16), v[cols, :])
                    al_run = al * al_run
                    m_run = m_new
                l_out.append(l_run)
                m_out.append(m_run)
                pv.append(pv_run)
                alpha.append(al_run)
            m_scr[...] = jnp.where(lane == 0, m_out[0], jnp.where(lane == 1, m_out[1], m_all))
            l_scr[...] = jnp.where(lane == 0, l_out[0], jnp.where(lane == 1, l_out[1], l_all))
            acc_scr[...] = acc_scr[...] * jnp.where(lo, alpha[0], alpha[1]) + jnp.where(lo, pv[0], pv[1])

        @pl.when(ki < qi)
        def _():
            step(False)

        @pl.when(ki == qi)
        def _():
            step(True)
            lo = lax.broadcasted_iota(jnp.int32, (t, PAIR_W), 1) < HEAD_DIM
            inv = jnp.where(lo, 1.0 / l_scr[:, 0:1], 1.0 / l_scr[:, 1:2])
            o_ref[...] = (acc_scr[...] * inv).astype(BF16)
            lse_ref[...] = m_scr[...] + jnp.log(l_scr[...])

    kv = lambda off: pl.BlockSpec((t, PAIR_W), lambda p, s, qt, kt: (kt[s], off + p))
    qblk = pl.BlockSpec((t, PAIR_W), lambda p, s, qt, kt: (qt[s], p))
    return pl.pallas_call(
        body, name=name,
        grid_spec=pltpu.PrefetchScalarGridSpec(
            num_scalar_prefetch=2, grid=(N_PAIRS, qtab.shape[0]),
            in_specs=[qblk, kv(N_PAIRS), kv(2 * N_PAIRS),
                      pl.BlockSpec((None, 2, t), lambda p, s, qt, kt: (p, 0, kt[s])), qblk],
            out_specs=[qblk, qblk],
            scratch_shapes=[pltpu.VMEM((t, PAIR_W), F32), pltpu.VMEM((t, PAIR_W), F32), pltpu.VMEM((t, PAIR_W), F32)]),
        out_shape=[SDS((lp, D_ATTN), BF16), SDS((lp, D_ATTN), F32)],
        compiler_params=_cparams(("parallel", "arbitrary"), 48),
    )(qtab, ktab, qkv, qkv, qkv, crow, cpair)


def _attn_bwd(qkv, do, crow, cpair, lse, delta, name):
    lp = qkv.shape[0]
    t = _row_tile(lp)
    nq = lp // t
    qtab, ktab = _causal_tiles(nq, key_major=True)

    def body(qt_ref, kt_ref, q_ref, k_ref, v_ref, do_ref, c_ref, cq_ref, lse_ref, dl_ref,
             dq_ref, dk_ref, dv_ref, dck_ref, dcq_ref, dq_scr, dcq_scr, dk_scr, dv_scr, dck_scr):
        step_id = pl.program_id(1)
        qi, ki = qt_ref[step_id], kt_ref[step_id]

        @pl.when(step_id == 0)
        def _():
            dq_scr[...] = jnp.zeros_like(dq_scr)
            dcq_scr[...] = jnp.zeros_like(dcq_scr)

        @pl.when(qi == ki)
        def _():
            dk_scr[...] = jnp.zeros_like(dk_scr)
            dv_scr[...] = jnp.zeros_like(dv_scr)
            dck_scr[...] = jnp.zeros_like(dck_scr)

        def step(diagonal):
            q, k, v, dout = q_ref[...], k_ref[...], v_ref[...], do_ref[...]
            lane = lax.broadcasted_iota(jnp.int32, (t, PAIR_W), 1)
            lo = lane < HEAD_DIM
            zero, one = jnp.zeros_like(q), jnp.ones_like(q)
            dq_new = jnp.zeros((t, PAIR_W), F32)
            dv_new = jnp.zeros((t, PAIR_W), F32)
            dk_new = jnp.zeros((t, PAIR_W), F32)
            col_sums, row_sums = [], []
            for j in range(2):
                sel = lo if j == 0 else jnp.logical_not(lo)
                qj, kj, doj = jnp.where(sel, q, zero), jnp.where(sel, k, zero), jnp.where(sel, dout, zero)
                p = jnp.exp(_dot_nt(qj, k) + (cq_ref[:, j:j + 1] - c_ref[j:j + 1, :]) - lse_ref[:, j:j + 1])
                if diagonal:
                    causal = (lax.broadcasted_iota(jnp.int32, (t, t), 0) >= lax.broadcasted_iota(jnp.int32, (t, t), 1))
                    p = jnp.where(causal, p, 0.0)
                dp = _dot_nt(doj, v)
                dsb = (p * (dp - dl_ref[:, j:j + 1])).astype(BF16)
                dv_new = dv_new + _dot_tn(p.astype(BF16), doj)
                ones_lane = HEAD_DIM * (1 - j)
                dk_j = _dot_tn(dsb, jnp.where(lane == ones_lane, one, qj))
                dq_j = _dot(dsb, jnp.where(lane == ones_lane, one, kj))
                dk_new = dk_new + jnp.where(sel, dk_j, 0.0)
                dq_new = dq_new + jnp.where(sel, dq_j, 0.0)
                col_sums.append(dk_j[:, ones_lane:ones_lane + 1])
                row_sums.append(dq_j[:, ones_lane:ones_lane + 1])
            dv_scr[...] += dv_new
            dk_scr[...] += dk_new
            dck_scr[...] += jnp.where(lane == 0, col_sums[0], jnp.where(lane == 1, col_sums[1], 0.0))
            rows = pl.ds(pl.multiple_of(qi * t, t), t)
            dq_scr[rows, :] += dq_new
            dcq_scr[rows, :] += jnp.where(lane == 0, row_sums[0], jnp.where(lane == 1, row_sums[1], 0.0))

        @pl.when(qi > ki)
        def _():
            step(False)

        @pl.when(qi == ki)
        def _():
            step(True)
            rows = pl.ds(pl.multiple_of(qi * t, t), t)
            dq_ref[...] = (dq_scr[rows, :] * (HEAD_DIM ** -0.5)).astype(BF16)
            dcq_ref[...] = dcq_scr[rows, :]

        @pl.when(qi == nq - 1)
        def _():
            dk_ref[...] = dk_scr[...].astype(BF16)
            dv_ref[...] = dv_scr[...].astype(BF16)
            dck_ref[...] = dck_scr[...]

    qside = lambda off: pl.BlockSpec((t, PAIR_W), lambda p, s, qt, kt: (qt[s], off + p))
    kside = lambda off: pl.BlockSpec((t, PAIR_W), lambda p, s, qt, kt: (kt[s], off + p))
    crow_blk = pl.BlockSpec((None, 2, t), lambda p, s, qt, kt: (p, 0, kt[s]))
    return pl.pallas_call(
        body, name=name,
        grid_spec=pltpu.PrefetchScalarGridSpec(
            num_scalar_prefetch=2, grid=(N_PAIRS, qtab.shape[0]),
            in_specs=[qside(0), kside(N_PAIRS), kside(2 * N_PAIRS), qside(0), crow_blk, qside(0), qside(0), qside(0)],
            out_specs=[kside(0), kside(0), kside(0), kside(0), kside(0)],
            scratch_shapes=[pltpu.VMEM((lp, PAIR_W), F32), pltpu.VMEM((lp, PAIR_W), F32), pltpu.VMEM((t, PAIR_W), F32),
                            pltpu.VMEM((t, PAIR_W), F32), pltpu.VMEM((t, PAIR_W), F32)]),
        out_shape=[SDS((lp, D_ATTN), BF16), SDS((lp, D_ATTN), BF16), SDS((lp, D_ATTN), BF16),
                   SDS((lp, D_ATTN), F32), SDS((lp, D_ATTN), F32)],
        compiler_params=_cparams(("parallel", "arbitrary"), 48),
    )(qtab, ktab, qkv, qkv, qkv, do, crow, cpair, lse, delta)


def _shard_sum(own, parts, name, slot_order=False):
    rows, cols = own.shape
    tr = rows if rows <= 512 else 512
    assert rows % tr == 0

    def body(me_ref, own_ref, parts_ref, o_ref):
        if slot_order:
            acc = parts_ref[0].astype(F32)
            for j in range(1, 4):
                acc = acc + parts_ref[j].astype(F32)
        else:
            me = me_ref[0]
            acc = own_ref[...]
            for j in range(4):
                acc = acc + jnp.where(me == j, 0.0, parts_ref[j].astype(F32))
        o_ref[...] = acc

    me = (2 * lax.axis_index("x") + lax.axis_index("y")).astype(jnp.int32).reshape(1)
    return pl.pallas_call(
        body, name=name,
        grid_spec=pltpu.PrefetchScalarGridSpec(
            num_scalar_prefetch=1, grid=(rows // tr,),
            in_specs=[pl.BlockSpec((tr, cols), lambda i, me: (i, 0)), pl.BlockSpec((4, tr, cols), lambda i, me: (0, i, 0))],
            out_specs=pl.BlockSpec((tr, cols), lambda i, me: (i, 0))),
        out_shape=SDS((rows, cols), F32),
        compiler_params=_cparams(("parallel",), 48),
    )(me, own, parts)


def _adamw(w, m, v, s_own, s_sib, name):
    rows, cols = w.shape
    tr = rows if rows <= 256 else 256
    assert rows % tr == 0

    def body(w_ref, m_ref, v_ref, a_ref, b_ref, g_ref, d_ref, nm_ref, nv_ref):
        g = a_ref[...] + b_ref[...]
        g_ref[...] = g
        m = ADAM_B1 * m_ref[...] + (1.0 - ADAM_B1) * g
        v = ADAM_B2 * v_ref[...] + (1.0 - ADAM_B2) * jnp.square(g)
        nm_ref[...] = m
        nv_ref[...] = v
        m_hat = m / (1.0 - ADAM_B1 ** ADAM_STEP)
        v_hat = v / (1.0 - ADAM_B2 ** ADAM_STEP)
        d_ref[...] = -ADAM_LR * (m_hat / (jnp.sqrt(v_hat) + ADAM_EPS) + ADAM_WD * w_ref[...])

    blk = pl.BlockSpec((tr, cols), lambda i: (i, 0))
    return pl.pallas_call(
        body, name=name, grid=(rows // tr,), in_specs=[blk] * 5, out_specs=[blk] * 4,
        out_shape=[SDS((rows, cols), F32)] * 4,
        compiler_params=_cparams(("parallel",), 48),
    )(w, m, v, s_own, s_sib)


def _pad_rows(a, rows):
    return jnp.pad(a, ((0, rows - a.shape[0]), (0, 0)))


def _flat_rows(arrs, cols, rows):
    flat = jnp.concatenate([a.reshape(-1) for a in arrs])
    return jnp.pad(flat, (0, rows * cols - flat.shape[0])).reshape(rows, cols)


def _unflat(buf, shapes):
    flat, out, off = buf.reshape(-1), [], 0
    for s in shapes:
        n = 1
        for d in s:
            n *= d
        out.append(flat[off:off + n].reshape(s))
        off += n
    return out


def kernel(x, meta_tokens, mix_norm_g, w_in, b_forget, w_conf_dw, b_conf_dw, conf_ln_g, conf_ln_b, w_conf_pw, b_conf_pw, w_sc_conv, w_out, mlp_norm_g, w_mlp1, w_mlp2, final_norm_g, loss_target, m_meta_tokens, m_mix_norm_g, m_w_in, m_b_forget, m_w_conf_dw, m_b_conf_dw, m_conf_ln_g, m_conf_ln_b, m_w_conf_pw, m_b_conf_pw, m_w_sc_conv, m_w_out, m_mlp_norm_g, m_w_mlp1, m_w_mlp2, m_final_norm_g, v_meta_tokens, v_mix_norm_g, v_w_in, v_b_forget, v_w_conf_dw, v_b_conf_dw, v_conf_ln_g, v_conf_ln_b, v_w_conf_pw, v_b_conf_pw, v_w_sc_conv, v_w_out, v_mlp_norm_g, v_w_mlp1, v_w_mlp2, v_final_norm_g):
    depth = w_in.shape[0]
    seq = x.shape[1]
    l_tok = N_META + seq
    lp = _padded_len(l_tok)
    n_in = w_in.shape[2] * 4
    n_in_sh = w_in.shape[2]
    me = 2 * lax.axis_index("x") + lax.axis_index("y")

    small_shapes = [w_conf_dw.shape, w_conf_pw.shape, w_sc_conv.shape, meta_tokens.shape]
    n_small = sum(a.size for a in (w_conf_dw, w_conf_pw, w_sc_conv, meta_tokens))
    small_rows = -(-n_small // 1024 // 8) * 8
    small = _flat_rows([w_conf_dw, w_conf_pw, w_sc_conv, meta_tokens], 1024, small_rows)
    g_in, g_out, g_m1, g_m2, g_small = _weights_all_gather(
        [w_in.astype(BF16).reshape(depth * D_MODEL, n_in_sh), w_out.astype(BF16).reshape(-1, D_MODEL),
         w_mlp1.astype(BF16).reshape(depth * D_MODEL, -1), w_mlp2.astype(BF16).reshape(-1, D_MODEL), small],
        name="weights_all_gather")
    w_in_f = jnp.concatenate([g_in[j].reshape(depth, D_MODEL, n_in_sh) for j in range(4)], axis=2)
    w_out_f = jnp.concatenate([g_out[j].reshape(depth, -1, D_MODEL) for j in range(4)], axis=1)
    w_m1_f = jnp.concatenate([g_m1[j].reshape(depth, D_MODEL, -1) for j in range(4)], axis=2)
    w_m2_f = jnp.concatenate([g_m2[j].reshape(depth, -1, D_MODEL) for j in range(4)], axis=1)
    sm = [_unflat(g_small[j], small_shapes) for j in range(4)]
    w_dw_f = jnp.concatenate([s[0] for s in sm], axis=2)
    w_pw_f = jnp.concatenate([s[1] for s in sm], axis=1)
    w_sc_f = jnp.concatenate([s[2] for s in sm], axis=2)
    meta_f = jnp.concatenate([s[3] for s in sm], axis=1)

    q_end, f_end = 3 * D_ATTN, 3 * D_ATTN + N_HEADS
    wqkv = [w_in_f[l, :, :q_end] for l in range(depth)]
    wrest = [jnp.concatenate([w_in_f[l, :, f_end:], w_in_f[l, :, q_end:f_end],
                              jnp.zeros((D_MODEL, N_F_PAD - N_HEADS), BF16)], axis=1) for l in range(depth)]
    wdw = [_pad_rows(w_dw_f[l], 32) for l in range(depth)]
    wsc = [_pad_rows(w_sc_f[l], 8) for l in range(depth)]
    wpw = [w_pw_f[l].astype(BF16) for l in range(depth)]
    vec = [_pad_rows(jnp.stack([b_conf_dw[l], conf_ln_g[l], conf_ln_b[l], b_conf_pw[l],
                                jnp.pad(b_forget[l], (0, D_CONF - N_HEADS))]), 8) for l in range(depth)]
    col = jnp.arange(D_ATTN)
    seg = (col[None, :] == ((col // PAIR_W) * PAIR_W + (col % PAIR_W) // HEAD_DIM)[:, None]).astype(F32)

    h = jnp.concatenate([meta_f, x[0], jnp.zeros((lp - l_tok, D_MODEL), F32)], axis=0)
    target = jnp.pad(loss_target[0], ((N_META, lp - l_tok), (0, 0)))
    saved = []
    for l in range(depth):
        g1, g2 = mix_norm_g[l][None, :], mlp_norm_g[l][None, :]
        hn1, qkv, rest = _proj_fwd(h, g1, wqkv[l], wrest[l], name=f"proj_fwd_{l}")
        cs, dwc, cv, ccol = _conv_fwd(rest, wdw[l], vec[l], wpw[l], wsc[l], name=f"conv_fwd_{l}")
        crow = ccol[:, :N_HEADS].T.reshape(N_PAIRS, 2, lp)
        cpair = jnp.pad(ccol[:, :N_HEADS].reshape(lp, N_PAIRS, 2), ((0, 0), (0, 0), (0, PAIR_W - 2))).reshape(lp, D_ATTN)
        attn, lse = _attn_fwd(qkv, crow, cpair, name=f"attn_fwd_{l}")
        h1 = _mix_out(h, attn, cs, w_out_f[l, :D_ATTN], w_out_f[l, D_ATTN:], name=f"mix_out_{l}")
        hn2, a, r, h2 = _mlp_fwd(h1, g2, w_m1_f[l], w_m2_f[l], name=f"mlp_fwd_{l}")
        saved.append((h, g1, g2, hn1, qkv, rest, cs, dwc, cv, crow, cpair, attn, lse, h1, hn2, a, r))
        h = h2
    dh, loss11, d_final_g = _loss_head(h, target, final_norm_g[None, :], seq, name="loss_head")
    loss = lax.psum(loss11[0, 0], ("x", "y", "c"))

    grads = [None] * depth
    for l in reversed(range(depth)):
        h0, g1, g2, hn1, qkv, rest, cs, dwc, cv, crow, cpair, attn, lse, h1, hn2, a, r = saved[l]
        da, dh1, dg2 = _mlp_bwd(dh, h1, g2, a, w_m1_f[l], w_m2_f[l], name=f"mlp_bwd_{l}")
        d_w2 = _mm_tn(r, dh, name=f"dw_mlp2_{l}")
        d_w1 = _mm_tn(hn2, da, name=f"dw_mlp1_{l}")
        do, dcs, delta = _mix_bwd(dh1, w_out_f[l, :D_ATTN], w_out_f[l, D_ATTN:], attn, seg, name=f"mix_bwd_{l}")
        d_wout = jnp.concatenate([_mm_tn(attn, dh1, name=f"dw_out_a_{l}"), _mm_tn(cs, dh1, name=f"dw_out_c_{l}")], axis=0)
        dq, dk, dv, dck, dcq = _attn_bwd(qkv, do, crow, cpair, lse, delta, name=f"attn_bwd_{l}")
        dc_heads = (dcq - dck).reshape(lp, N_PAIRS, PAIR_W)[:, :, :2].reshape(lp, N_HEADS)
        dccol = jnp.pad(dc_heads, ((0, 0), (0, N_F_PAD - N_HEADS)))
        drest, d_wdw, d_wsc, d_wpw, d_vec = _conv_bwd(dcs, dwc, cv, rest, dccol, wdw[l], vec[l], wpw[l], wsc[l],
                                                      name=f"conv_bwd_{l}")
        dh, dg1 = _in_bwd(dh1, h0, g1, dq, dk, dv, drest, wqkv[l], wrest[l], name=f"in_bwd_{l}")
        d_wq = _mm_tn(hn1, dq, name=f"dw_q_{l}")
        d_wk = _mm_tn(hn1, dk, name=f"dw_k_{l}")
        d_wv = _mm_tn(hn1, dv, name=f"dw_v_{l}")
        d_wr = _mm_tn(hn1, drest, name=f"dw_rest_{l}")
        d_win = jnp.concatenate([d_wq, d_wk, d_wv, d_wr[:, 5 * D_CONF:5 * D_CONF + N_HEADS], d_wr[:, :5 * D_CONF]], axis=1)
        grads[l] = dict(w_in=d_win, w_out=d_wout, w_mlp1=d_w1, w_mlp2=d_w2, mix_norm_g=dg1[0], mlp_norm_g=dg2[0],
                        w_conf_dw=d_wdw[:CONF_K], w_sc_conv=d_wsc[:SC_K], w_conf_pw=d_wpw, b_conf_dw=d_vec[0],
                        conf_ln_g=d_vec[1], conf_ln_b=d_vec[2], b_conf_pw=d_vec[3], b_forget=d_vec[4, :N_HEADS])
    grad_x = dh[N_META:N_META + seq][None]
    d_meta = dh[:N_META]

    def stack(k):
        return jnp.stack([grads[l][k] for l in range(depth)])

    def col_shards(a):
        d, rr, cc = a.shape
        return a.reshape(d, rr, 4, cc // 4).transpose(2, 0, 1, 3).reshape(4, d * rr, cc // 4)

    def row_shards(a):
        d, rr, cc = a.shape
        return a.reshape(d, 4, rr // 4, cc).transpose(1, 0, 2, 3).reshape(4, d * rr // 4, cc)

    big = [col_shards(stack("w_in")), row_shards(stack("w_out")), col_shards(stack("w_mlp1")), row_shards(stack("w_mlp2"))]
    rep = [stack("mix_norm_g"), stack("b_forget"), stack("b_conf_dw"), stack("conf_ln_g"), stack("conf_ln_b"),
           stack("b_conf_pw"), stack("mlp_norm_g"), d_final_g[0]]
    rep_shapes = [a.shape for a in rep]
    sh_dw, sh_pw, sh_sc = col_shards(stack("w_conf_dw")), row_shards(stack("w_conf_pw")), col_shards(stack("w_sc_conv"))
    sh_meta = d_meta.reshape(N_META, 4, D_MODEL // 4).transpose(1, 0, 2)
    n_rep = sum(a.size for a in rep)
    gs_rows = -(-(n_small + n_rep) // 1024 // 8) * 8
    gsmall = jnp.stack([_flat_rows([sh_dw[j], sh_pw[j], sh_sc[j], sh_meta[j]] + rep, 1024, gs_rows) for j in range(4)])
    recv = _chip_exchange([b.astype(BF16) for b in big] + [gsmall], name="grads_exchange")
    own = [lax.dynamic_index_in_dim(b, me, 0, keepdims=False) for b in big + [gsmall]]
    sums = [_shard_sum(own[i], recv[i], name=f"shard_sum_{i}", slot_order=(i == 4)) for i in range(5)]
    sibs = _sibling_exchange(sums, name="sibling_exchange")

    w_small = _flat_rows([w_conf_dw, w_conf_pw, w_sc_conv, meta_tokens, mix_norm_g, b_forget, b_conf_dw, conf_ln_g,
                          conf_ln_b, b_conf_pw, mlp_norm_g, final_norm_g], 1024, gs_rows)
    m_small = _flat_rows([m_w_conf_dw, m_w_conf_pw, m_w_sc_conv, m_meta_tokens, m_mix_norm_g, m_b_forget, m_b_conf_dw,
                          m_conf_ln_g, m_conf_ln_b, m_b_conf_pw, m_mlp_norm_g, m_final_norm_g], 1024, gs_rows)
    v_small = _flat_rows([v_w_conf_dw, v_w_conf_pw, v_w_sc_conv, v_meta_tokens, v_mix_norm_g, v_b_forget, v_b_conf_dw,
                          v_conf_ln_g, v_conf_ln_b, v_b_conf_pw, v_mlp_norm_g, v_final_norm_g], 1024, gs_rows)
    triples = [(w_in, m_w_in, v_w_in), (w_out, m_w_out, v_w_out), (w_mlp1, m_w_mlp1, v_w_mlp1),
               (w_mlp2, m_w_mlp2, v_w_mlp2), (w_small, m_small, v_small)]
    res = []
    for i, (w, m, v) in enumerate(triples):
        shp = w.shape
        two_d = sums[i].shape
        out4 = _adamw(w.reshape(two_d), m.reshape(two_d), v.reshape(two_d), sums[i], sibs[i], name=f"adamw_{i}")
        res.append([o.reshape(shp) for o in out4])
    small_out = [_unflat(res[4][k], small_shapes + rep_shapes) for k in range(4)]

    names = ["meta_tokens", "mix_norm_g", "w_in", "b_forget", "w_conf_dw", "b_conf_dw", "conf_ln_g", "conf_ln_b",
             "w_conf_pw", "b_conf_pw", "w_sc_conv", "w_out", "mlp_norm_g", "w_mlp1", "w_mlp2", "final_norm_g"]
    small_order = ["w_conf_dw", "w_conf_pw", "w_sc_conv", "meta_tokens", "mix_norm_g", "b_forget", "b_conf_dw",
                   "conf_ln_g", "conf_ln_b", "b_conf_pw", "mlp_norm_g", "final_norm_g"]
    big_order = {"w_in": 0, "w_out": 1, "w_mlp1": 2, "w_mlp2": 3}
    outs = []
    for k in range(4):
        for nme in names:
            outs.append(res[big_order[nme]][k] if nme in big_order else small_out[k][small_order.index(nme)])
    return (loss, grad_x, *outs)
```
